```python
import functools
import jax, jax.numpy as jnp
from jax import lax
import numpy as np

D_MODEL = 2048
BATCH = 2
SEQ = 4096
DEPTH = 4
DEC_BATCH = 8
DEC_SEQ = 16
PAST_LEN = 4096

CHUNK = 64
CONV_DIM = D_MODEL // 4
CONV_WIDTH = 31
SWA_HEAD_DIM = 64
SWA_DIM = D_MODEL // 2
SWA_HEADS = SWA_DIM // SWA_HEAD_DIM
SWA_KV_HEADS = 2
SWA_GROUP = SWA_HEADS // SWA_KV_HEADS
SWA_KV_DIM = SWA_KV_HEADS * SWA_HEAD_DIM
WINDOW = 128
WIN_CHUNKS = WINDOW // CHUNK
MEM_TOKENS = 256
MEM_HEADS = 4
MEM_DIM = D_MODEL // 4
MEM_HEAD_DIM = MEM_DIM // MEM_HEADS
MIX_DIM = CONV_DIM + SWA_DIM + MEM_DIM
NUM_BUCKETS = 32
MAX_DISTANCE = 128
EPS = 1e-6
NEG_INF = -1e30
SWA_SCALE = SWA_HEAD_DIM ** -0.5
MEM_SCALE = MEM_HEAD_DIM ** -0.5
IN_SPLITS = (CONV_DIM, CONV_DIM, CONV_DIM, SWA_DIM, SWA_KV_DIM, SWA_KV_DIM, SWA_DIM, MEM_DIM, MEM_DIM)
IN_DIM = sum(IN_SPLITS)

kernel_name = "hybrid_conv_swa_memory_stream_step"


def rms_norm(x, g):
    xf = x.astype(jnp.float32)
    y = xf * lax.rsqrt(jnp.mean(xf * xf, axis=-1, keepdims=True) + EPS)
    return (y * g.astype(jnp.float32)).astype(x.dtype)


def layer_norm(x, g, b):
    xf = x.astype(jnp.float32)
    mu = jnp.mean(xf, axis=-1, keepdims=True)
    xc = xf - mu
    y = xc * lax.rsqrt(jnp.mean(xc * xc, axis=-1, keepdims=True) + EPS)
    return (y * g.astype(jnp.float32) + b.astype(jnp.float32)).astype(x.dtype)


def t5_bucket(rel):
    half = NUM_BUCKETS // 2
    exact = half // 2
    side = jnp.where(rel > 0, half, 0)
    n = jnp.abs(rel)
    nf = jnp.maximum(n, 1).astype(jnp.float32)
    large = exact + (jnp.log(nf / exact) / np.float32(np.log(MAX_DISTANCE / exact)) * (half - exact)).astype(jnp.int32)
    large = jnp.minimum(large, half - 1)
    return side + jnp.where(n < exact, n, large)


def rel_pos_bias(table, n_q, n_k, offset):
    rel = jnp.arange(n_k)[None, :] - offset - jnp.arange(n_q)[:, None]
    b = jnp.take(table, t5_bucket(rel), axis=0)
    b = jnp.transpose(b, (2, 0, 1)).astype(jnp.float32)
    return b.reshape(SWA_KV_HEADS, SWA_GROUP, n_q, n_k)


def sink_attention(q, k, v, bias, mask, sinks):
    s = jnp.einsum('...ikgd,...jkd->...kgij', q, k).astype(jnp.float32) * SWA_SCALE + bias
    if mask is not None:
        s = jnp.where(mask, s, NEG_INF)
    sink = sinks.astype(jnp.float32).reshape(SWA_KV_HEADS, SWA_GROUP, 1, 1)
    m = jnp.maximum(jnp.max(s, axis=-1, keepdims=True), sink)
    p = jnp.exp(s - m)
    w = p / (jnp.sum(p, axis=-1, keepdims=True) + jnp.exp(sink - m))
    return jnp.einsum('...kgij,...jkd->...ikgd', w.astype(v.dtype), v)


def swa_prompt(q, k, v, sinks, bias):
    B, T = q.shape[0], q.shape[1]
    nC = T // CHUNK
    pad = WIN_CHUNKS * CHUNK
    qb = q.reshape(B, nC, CHUNK, SWA_KV_HEADS, SWA_GROUP, SWA_HEAD_DIM)

    def band(t):
        tp = jnp.pad(t, ((0, 0), (pad, 0), (0, 0), (0, 0)))
        tp = tp.reshape(B, nC + WIN_CHUNKS, CHUNK, SWA_KV_HEADS, SWA_HEAD_DIM)
        return jnp.concatenate([tp[:, w:w + nC] for w in range(WIN_CHUNKS + 1)], axis=2)

    kb, vb = band(k), band(v)
    key_pos = jnp.arange(nC)[:, None] * CHUNK + jnp.arange(pad + CHUNK)[None, :] - pad
    mask = (key_pos >= 0)[:, None, None, None, :]
    o = sink_attention(qb, kb, vb, bias, mask, sinks)
    return o.reshape(B, T, SWA_DIM), k[:, -WINDOW:], v[:, -WINDOW:]


def swa_sample(q, k, v, cache_k, cache_v, sinks, bias):
    B, S = q.shape[0], q.shape[1]
    L = cache_k.shape[1]
    qh = q.reshape(B, S, SWA_KV_HEADS, SWA_GROUP, SWA_HEAD_DIM)
    kf = jnp.concatenate([cache_k, k], axis=1)
    vf = jnp.concatenate([cache_v, v], axis=1)
    o = sink_attention(qh, kf, vf, bias, None, sinks)
    return o.reshape(B, S, SWA_DIM), kf[:, -L:], vf[:, -L:]


def memory_kv(mem, mem_g, w_mem_kv):
    B, M = mem.shape[0], mem.shape[1]
    kv = rms_norm(mem, mem_g) @ w_mem_kv
    mk, mv = jnp.split(kv, 2, axis=-1)
    return (mk.reshape(B, M, MEM_HEADS, MEM_HEAD_DIM), mv.reshape(B, M, MEM_HEADS, MEM_HEAD_DIM))


def mem_attention(q, mk, mv):
    B, T = q.shape[0], q.shape[1]
    qh = q.reshape(B, T, MEM_HEADS, MEM_HEAD_DIM)
    s = jnp.einsum('bthd,bmhd->bhtm', qh, mk).astype(jnp.float32) * MEM_SCALE
    p = jax.nn.softmax(s, axis=-1).astype(mv.dtype)
    return jnp.einsum('bhtm,bmhd->bthd', p, mv).reshape(B, T, MEM_DIM)


def causal_dwconv(x_padded, w, b):
    y = lax.conv_general_dilated(x_padded, w[:, None, :], window_strides=(1,), padding='VALID',
                                 dimension_numbers=('NWC', 'WIO', 'NWC'), feature_group_count=CONV_DIM)
    return y + b


def mixer_layer(x, conv_left, swa_fn, mem_k, mem_v, norm_g, w_in, conv_w, conv_b,
                conv_ln_g, conv_ln_b, w_pw, b_pw, w_out):
    B, T = x.shape[0], x.shape[1]
    h = rms_norm(x, norm_g)
    z = h @ w_in
    pts = []
    acc = 0
    for wdt in IN_SPLITS[:-1]:
        acc += wdt
        pts.append(acc)
    a_val, a_glu, a_gate, bq, bk, bv, b_gate, cq, c_gate = jnp.split(z, pts, axis=-1)
    u = a_val * jax.nn.sigmoid(a_glu)
    conv_in = jnp.concatenate([conv_left, u], axis=1)
    new_conv = conv_in[:, -(CONV_WIDTH - 1):]
    c = causal_dwconv(conv_in, conv_w, conv_b)
    c = jax.nn.silu(layer_norm(c, conv_ln_g, conv_ln_b))
    c = c @ w_pw + b_pw
    y_a = c * jax.nn.silu(a_gate)
    k = bk.reshape(B, T, SWA_KV_HEADS, SWA_HEAD_DIM)
    v = bv.reshape(B, T, SWA_KV_HEADS, SWA_HEAD_DIM)
    o_b, new_k, new_v = swa_fn(bq, k, v)
    y_b = o_b * jax.nn.silu(b_gate)
    y_c = mem_attention(cq, mem_k, mem_v) * jax.nn.silu(c_gate)
    y = jnp.concatenate([y_a, y_b, y_c], axis=-1) @ w_out
    return x + y, new_conv, new_k, new_v


def setup_inputs(seed: int = 0) -> dict:
    key = jax.random.key(seed)
    ks = jax.random.split(key, 24)
    f32 = jnp.float32
    n = lambda k, s: jax.random.normal(k, s, f32)
    n_win = min(WINDOW, PAST_LEN)
    return {
        "x_prompt": n(ks[0], (BATCH, SEQ, D_MODEL)),
        "x_sample": n(ks[1], (DEC_BATCH, DEC_SEQ, D_MODEL)),
        "mem_prompt": n(ks[2], (BATCH, MEM_TOKENS, D_MODEL)),
        "cache_swa_k": n(ks[3], (DEPTH, DEC_BATCH, n_win, SWA_KV_HEADS, SWA_HEAD_DIM)),
        "cache_swa_v": n(ks[4], (DEPTH, DEC_BATCH, n_win, SWA_KV_HEADS, SWA_HEAD_DIM)),
        "cache_conv": 0.5 * n(ks[5], (DEPTH, DEC_BATCH, CONV_WIDTH - 1, CONV_DIM)),
        "cache_mem_k": n(ks[6], (DEPTH, DEC_BATCH, MEM_TOKENS, MEM_HEADS, MEM_HEAD_DIM)),
        "cache_mem_v": n(ks[7], (DEPTH, DEC_BATCH, MEM_TOKENS, MEM_HEADS, MEM_HEAD_DIM)),
        "norm_g": 1.0 + 0.02 * n(ks[8], (DEPTH, D_MODEL)),
        "w_in": n(ks[9], (DEPTH, D_MODEL, IN_DIM)) * D_MODEL ** -0.5,
        "conv_w": n(ks[10], (DEPTH, CONV_WIDTH, CONV_DIM)) * CONV_WIDTH ** -0.5,
        "conv_b": 0.02 * n(ks[11], (DEPTH, CONV_DIM)),
        "conv_ln_g": 1.0 + 0.02 * n(ks[12], (DEPTH, CONV_DIM)),
        "conv_ln_b": 0.02 * n(ks[13], (DEPTH, CONV_DIM)),
        "w_pw": n(ks[14], (DEPTH, CONV_DIM, CONV_DIM)) * CONV_DIM ** -0.5,
        "b_pw": 0.02 * n(ks[15], (DEPTH, CONV_DIM)),
        "swa_sinks": n(ks[16], (DEPTH, SWA_HEADS)),
        "mem_norm_g": 1.0 + 0.02 * n(ks[17], (DEPTH, D_MODEL)),
        "w_mem_kv": n(ks[18], (DEPTH, D_MODEL, 2 * MEM_DIM)) * D_MODEL ** -0.5,
        "rel_bias": 0.5 * n(ks[19], (NUM_BUCKETS, SWA_HEADS)),
        "w_out": n(ks[20], (DEPTH, MIX_DIM, D_MODEL)) * MIX_DIM ** -0.5,
        "final_norm_g": 1.0 + 0.02 * n(ks[21], (D_MODEL,)),
    }


def reference(x_prompt, x_sample, mem_prompt, cache_swa_k, cache_swa_v, cache_conv, cache_mem_k,
              cache_mem_v, norm_g, w_in, conv_w, conv_b, conv_ln_g, conv_ln_b, w_pw, b_pw,
              swa_sinks, mem_norm_g, w_mem_kv, rel_bias, w_out, final_norm_g):
    B, T = x_prompt.shape[0], x_prompt.shape[1]
    S = x_sample.shape[1]
    L = cache_swa_k.shape[2]
    band = WIN_CHUNKS * CHUNK
    bias_p = rel_pos_bias(rel_bias, CHUNK, band + CHUNK, band)
    bias_s = rel_pos_bias(rel_bias, S, L + S, L)
    hp, hs = x_prompt, x_sample
    p_k, p_v, p_conv, p_mk, p_mv = [], [], [], [], []
    s_k, s_v, s_conv = [], [], []
    for l in range(DEPTH):
        lw = (norm_g[l], w_in[l], conv_w[l], conv_b[l], conv_ln_g[l], conv_ln_b[l], w_pw[l], b_pw[l], w_out[l])
        mk, mv = memory_kv(mem_prompt, mem_norm_g[l], w_mem_kv[l])
        left = jnp.zeros((B, CONV_WIDTH - 1, CONV_DIM), hp.dtype)
        fn_p = functools.partial(swa_prompt, sinks=swa_sinks[l], bias=bias_p)
        hp, c_new, k_new, v_new = mixer_layer(hp, left, fn_p, mk, mv, *lw)
        p_k.append(k_new)
        p_v.append(v_new)
        p_conv.append(c_new)
        p_mk.append(mk)
        p_mv.append(mv)
        fn_s = functools.partial(swa_sample, cache_k=cache_swa_k[l], cache_v=cache_swa_v[l],
                                 sinks=swa_sinks[l], bias=bias_s)
        hs, c_new, k_new, v_new = mixer_layer(hs, cache_conv[l], fn_s, cache_mem_k[l], cache_mem_v[l], *lw)
        s_k.append(k_new)
        s_v.append(v_new)
        s_conv.append(c_new)
    y_prompt = rms_norm(hp, final_norm_g)
    y_sample = rms_norm(hs, final_norm_g)
    return (y_prompt, y_sample, jnp.stack(p_k), jnp.stack(p_v), jnp.stack(p_conv), jnp.stack(p_mk),
            jnp.stack(p_mv), jnp.stack(s_k), jnp.stack(s_v), jnp.stack(s_conv))
```

```python
import functools

import numpy as np
import jax
import jax.numpy as jnp
from jax import lax
from jax.experimental import pallas as pl
from jax.experimental.pallas import tpu as pltpu

F32 = jnp.float32
BF16 = jnp.bfloat16

D_MODEL = 2048
DEPTH = 4
CHUNK = 64
CONV_DIM = 512
CONV_WIDTH = 31
CONV_HIST = CONV_WIDTH - 1
SWA_HEAD_DIM = 64
SWA_DIM = 1024
SWA_HEADS = 16
SWA_KV_HEADS = 2
SWA_KV_DIM = SWA_KV_HEADS * SWA_HEAD_DIM
SWA_PAIRS = 4
WINDOW = 128
MEM_TOKENS = 256
MEM_HEADS = 4
MEM_DIM = 512
MEM_HEAD_DIM = 128
NUM_BUCKETS = 32
MAX_DISTANCE = 128
EPS = 1e-6
NEG_INF = -1e30
SWA_SCALE = SWA_HEAD_DIM ** -0.5
MEM_SCALE = MEM_HEAD_DIM ** -0.5

A_COLS = 3 * CONV_DIM
B_COLS = SWA_DIM + 2 * SWA_KV_DIM + SWA_DIM
C_COLS = 2 * MEM_DIM
B_K0 = SWA_DIM
B_V0 = SWA_DIM + SWA_KV_DIM
B_G0 = SWA_DIM + 2 * SWA_KV_DIM
KEY_PAD = 256
QBLK = 128
HIST_ROWS = 32
TQ = 256
CONV_RB = 32
VMEM_LIMIT = 60 * 1024 * 1024

_NT = (((1,), (1,)), ((), ()))


def _sigmoid(x):
    return 1.0 / (1.0 + jnp.exp(-x))


def _silu(x):
    return x * _sigmoid(x)


def _rms(x, g):
    return x * lax.rsqrt(jnp.mean(x * x, axis=-1, keepdims=True) + EPS) * g


def _const_spec(shape):
    nd = len(shape)
    return pl.BlockSpec(shape, lambda *_: (0,) * nd, pipeline_mode=pl.Buffered(1))


def _t5_bucket(rel):
    half = NUM_BUCKETS // 2
    exact = half // 2
    side = jnp.where(rel > 0, half, 0)
    n = jnp.abs(rel)
    nf = jnp.maximum(n, 1).astype(jnp.float32)
    large = exact + (jnp.log(nf / exact) / np.float32(np.log(MAX_DISTANCE / exact)) * (half - exact)).astype(jnp.int32)
    large = jnp.minimum(large, half - 1)
    return side + jnp.where(n < exact, n, large)


def _bias_kernel(table_ref, bucket_ref, valid_ref, out_ref):
    h = pl.program_id(0)
    bucket = bucket_ref[...]
    acc = jnp.zeros(bucket.shape, F32)
    for b in range(NUM_BUCKETS):
        acc = jnp.where(bucket == b, table_ref[b, h], acc)
    for v in range(valid_ref.shape[0]):
        out_ref[v, 0] = jnp.where(valid_ref[v] != 0, acc, NEG_INF)


def _bias_tables(rel_bias, n_q, valid):
    rel = jnp.arange(KEY_PAD)[None, :] - WINDOW - jnp.arange(n_q)[:, None]
    bucket = _t5_bucket(rel).astype(jnp.int32)
    nv = valid.shape[0]
    raw = pl.pallas_call(
        _bias_kernel,
        out_shape=jax.ShapeDtypeStruct((nv, SWA_HEADS, n_q, KEY_PAD), F32),
        grid=(SWA_HEADS,),
        in_specs=[
            pl.BlockSpec(memory_space=pltpu.SMEM),
            pl.BlockSpec((n_q, KEY_PAD), lambda h: (0, 0)),
            pl.BlockSpec((nv, n_q, KEY_PAD), lambda h: (0, 0, 0)),
        ],
        out_specs=pl.BlockSpec((nv, 1, n_q, KEY_PAD), lambda h: (0, h, 0, 0)),
        name="bias_tables",
    )(rel_bias, bucket, jnp.asarray(valid, jnp.int32))
    raw = raw.reshape(nv, SWA_KV_HEADS, SWA_PAIRS, 2, n_q, KEY_PAD)
    return raw.transpose(0, 1, 2, 4, 3, 5).reshape(nv, SWA_KV_HEADS, SWA_PAIRS * n_q, 2 * KEY_PAD)


def _prompt_valid():
    i = np.arange(QBLK)[:, None]
    j = np.arange(KEY_PAD)[None, :]
    kc = j // CHUNK - WINDOW // CHUNK
    qc = i // CHUNK
    band = (kc <= qc) & (kc >= qc - WINDOW // CHUNK)
    first = band & (j >= WINDOW)
    return np.stack([first, band]).astype(np.int32)


def _sample_valid(n_q, n_keys):
    j = np.arange(KEY_PAD)[None, :]
    return np.broadcast_to(j < n_keys, (1, n_q, KEY_PAD)).astype(np.int32)


def _conv_post(c, lg, lb, wpw, bpw, gate):
    mu = jnp.mean(c, axis=-1, keepdims=True)
    xc = c - mu
    y = xc * lax.rsqrt(jnp.mean(xc * xc, axis=-1, keepdims=True) + EPS) * lg + lb
    y = _silu(y)
    y = jnp.dot(y.astype(BF16), wpw, preferred_element_type=F32) + bpw
    return y * _silu(gate)


def _split_halves(x):
    xr = pltpu.roll(x, SWA_HEAD_DIM, axis=1)
    lo = lax.broadcasted_iota(jnp.int32, x.shape, 1) < SWA_HEAD_DIM
    zero = jnp.zeros_like(x)
    return (jnp.where(lo, x, zero).astype(BF16), jnp.where(lo, zero, x).astype(BF16),
            jnp.where(lo, xr, zero).astype(BF16), jnp.where(lo, zero, xr).astype(BF16))


def _swa_attend(lhs4, kp, vc, bias, sink_fn, nq):
    s = lax.dot_general(lhs4, kp, _NT, preferred_element_type=F32) + bias
    lane_lo = lax.broadcasted_iota(jnp.int32, (nq, 2 * SWA_HEAD_DIM), 1) < SWA_HEAD_DIM
    prow, inv_rows = [], []
    for p in range(SWA_PAIRS):
        es, invs = [], []
        for par in range(2):
            sh = s[p * nq:(p + 1) * nq, par * KEY_PAD:(par + 1) * KEY_PAD]
            sink = sink_fn(p, par)
            m = jnp.maximum(jnp.max(sh, axis=1, keepdims=True), sink)
            e = jnp.exp(sh - m)
            den = jnp.sum(e, axis=1, keepdims=True) + jnp.exp(sink - m)
            es.append(e.astype(BF16))
            invs.append(1.0 / den)
        prow.append(jnp.concatenate(es, axis=1))
        inv_rows.append(jnp.where(lane_lo, invs[0], invs[1]))
    pm = jnp.concatenate(prow, axis=0)
    o = jnp.dot(pm, vc, preferred_element_type=F32)
    return o * jnp.concatenate(inv_rows, axis=0)


def _mem_attend(cq, mk, mv):
    outs = []
    for hh in range(MEM_HEADS):
        cs = slice(hh * MEM_HEAD_DIM, (hh + 1) * MEM_HEAD_DIM)
        s = lax.dot_general(cq[:, cs].astype(BF16), mk[:, cs], _NT, preferred_element_type=F32) * MEM_SCALE
        m = jnp.max(s, axis=1, keepdims=True)
        e = jnp.exp(s - m)
        p = e / jnp.sum(e, axis=1, keepdims=True)
        outs.append(jnp.dot(p.astype(BF16), mv[:, cs], preferred_element_type=F32))
    return jnp.concatenate(outs, axis=1)


def _memkv_kernel(mem_ref, g_ref, w_ref, mk_ref, mv_ref):
    h = _rms(mem_ref[...], g_ref[0]).astype(BF16)
    kv = jnp.dot(h, w_ref[0], preferred_element_type=F32)
    mk_ref[0] = kv[:, :MEM_DIM]
    mv_ref[0] = kv[:, MEM_DIM:]


def _memory_kv(mem2d, mem_norm_g, w_mem_kv_bf):
    rows = mem2d.shape[0]
    return pl.pallas_call(
        _memkv_kernel,
        out_shape=(jax.ShapeDtypeStruct((DEPTH, rows, MEM_DIM), F32),
                   jax.ShapeDtypeStruct((DEPTH, rows, MEM_DIM), F32)),
        grid=(DEPTH,),
        in_specs=[
            pl.BlockSpec((rows, D_MODEL), lambda l: (0, 0)),
            pl.BlockSpec((1, 1, D_MODEL), lambda l: (l, 0, 0)),
            pl.BlockSpec((1, D_MODEL, 2 * MEM_DIM), lambda l: (l, 0, 0)),
        ],
        out_specs=(pl.BlockSpec((1, rows, MEM_DIM), lambda l: (l, 0, 0)),
                   pl.BlockSpec((1, rows, MEM_DIM), lambda l: (l, 0, 0))),
        compiler_params=pltpu.CompilerParams(dimension_semantics=("arbitrary",), vmem_limit_bytes=VMEM_LIMIT),
        name="memory_kv",
    )(mem2d, mem_norm_g.reshape(DEPTH, 1, D_MODEL), w_mem_kv_bf)


def _prompt_kernel(sinks_ref, x_ref, ng_ref, wa_ref, wb_ref, wc_ref, cw_ref, cb_ref, lg_ref, lb_ref, wpw_ref,
                   bpw_ref, bias_ref, mk_ref, mv_ref, wo_ref, fg_ref,
                   out_ref, nk_ref, nv_ref, nc_ref,
                   h_s, z_s, y_s, c_s, u_s, k4_s, v4_s, *, final):
    t = pl.program_id(1)
    last_t = pl.num_programs(1) - 1

    @pl.when(t == 0)
    def _():
        u_s[0:HIST_ROWS, :] = jnp.zeros((HIST_ROWS, CONV_DIM), F32)
        k4_s[:, 0:WINDOW, :] = jnp.zeros((4, WINDOW, 2 * SWA_HEAD_DIM), BF16)
        v4_s[:, 0:WINDOW, :] = jnp.zeros((4, WINDOW, 2 * SWA_HEAD_DIM), BF16)

    x = x_ref[0]
    h_s[...] = _rms(x, ng_ref[...]).astype(BF16)

    z_s[:, 0:A_COLS] = jnp.dot(h_s[...], wa_ref[...], preferred_element_type=F32)
    u_s[HIST_ROWS:HIST_ROWS + TQ, :] = z_s[:, 0:CONV_DIM] * _sigmoid(z_s[:, CONV_DIM:2 * CONV_DIM])

    @pl.when(t == last_t)
    def _():
        nc_ref[0] = u_s[HIST_ROWS + TQ - CONV_HIST:HIST_ROWS + TQ, :]

    off0 = HIST_ROWS - CONV_HIST
    for rb in range(TQ // CONV_RB):
        r0 = rb * CONV_RB
        acc = jnp.zeros((CONV_RB, CONV_DIM), F32) + cb_ref[...]
        for k in range(CONV_WIDTH):
            acc = acc + cw_ref[k:k + 1, :] * u_s[r0 + off0 + k:r0 + off0 + k + CONV_RB, :]
        c_s[r0:r0 + CONV_RB, :] = acc
    u_s[0:HIST_ROWS, :] = u_s[TQ:TQ + HIST_ROWS, :]
    ya = _conv_post(c_s[...], lg_ref[...], lb_ref[...], wpw_ref[...], bpw_ref[...],
                    z_s[:, 2 * CONV_DIM:3 * CONV_DIM])
    y_s[:, 0:CONV_DIM] = ya.astype(BF16)

    z_s[:, 0:B_COLS] = jnp.dot(h_s[...], wb_ref[...], preferred_element_type=F32)
    k_new = z_s[:, B_K0:B_K0 + SWA_KV_DIM]
    v_new = z_s[:, B_V0:B_V0 + SWA_KV_DIM]

    @pl.when(t == last_t)
    def _():
        nk_ref[0] = k_new[TQ - WINDOW:, :]
        nv_ref[0] = v_new[TQ - WINDOW:, :]

    for i, part in enumerate(_split_halves(k_new)):
        k4_s[i, WINDOW:WINDOW + TQ, :] = part
    for i, part in enumerate(_split_halves(v_new)):
        v4_s[i, WINDOW:WINDOW + TQ, :] = part

    for blk in range(TQ // QBLK):
        q0 = blk * QBLK
        band = slice(q0, q0 + KEY_PAD)
        variant = jnp.where(t == 0, 0, 1) if blk == 0 else 1
        for kv in range(SWA_KV_HEADS):
            c0 = kv * SWA_PAIRS * 2 * SWA_HEAD_DIM
            lhs4 = jnp.concatenate(
                [(z_s[q0:q0 + QBLK, c0 + p * 128:c0 + (p + 1) * 128] * SWA_SCALE).astype(BF16)
                 for p in range(SWA_PAIRS)], axis=0)
            if kv == 0:
                kp = jnp.concatenate([k4_s[0, band, :], k4_s[3, band, :]], axis=0)
                vc = jnp.concatenate([v4_s[0, band, :], v4_s[3, band, :]], axis=0)
            else:
                kp = jnp.concatenate([k4_s[2, band, :], k4_s[1, band, :]], axis=0)
                vc = jnp.concatenate([v4_s[2, band, :], v4_s[1, band, :]], axis=0)
            sink_fn = lambda p, par, kv=kv: sinks_ref[kv * 2 * SWA_PAIRS + 2 * p + par]
            o = _swa_attend(lhs4, kp, vc, bias_ref[variant, kv], sink_fn, QBLK)
            for p in range(SWA_PAIRS):
                gate = z_s[q0:q0 + QBLK, B_G0 + c0 + p * 128:B_G0 + c0 + (p + 1) * 128]
                yb = o[p * QBLK:(p + 1) * QBLK, :] * _silu(gate)
                y_s[q0:q0 + QBLK, CONV_DIM + c0 + p * 128:CONV_DIM + c0 + (p + 1) * 128] = yb.astype(BF16)
    k4_s[:, 0:WINDOW, :] = k4_s[:, TQ:TQ + WINDOW, :]
    v4_s[:, 0:WINDOW, :] = v4_s[:, TQ:TQ + WINDOW, :]

    z_s[:, 0:C_COLS] = jnp.dot(h_s[...], wc_ref[...], preferred_element_type=F32)
    yc = _mem_attend(z_s[:, 0:MEM_DIM], mk_ref[0].astype(BF16), mv_ref[0].astype(BF16))
    y_s[:, CONV_DIM + SWA_DIM:] = (yc * _silu(z_s[:, MEM_DIM:C_COLS])).astype(BF16)

    o = x + jnp.dot(y_s[...], wo_ref[...], preferred_element_type=F32)
    if final:
        o = _rms(o, fg_ref[...])
    out_ref[0] = o


def _prompt_layer(x, sinks, ng, wa, wb, wc, cw, cb, lg, lb, wpw, bpw, bias, mk, mv, wo, fg, final):
    B, T, _ = x.shape
    nt = T // TQ
    row = lambda n: _const_spec((1, n))
    return pl.pallas_call(
        functools.partial(_prompt_kernel, final=final),
        out_shape=(jax.ShapeDtypeStruct((B, T, D_MODEL), F32),
                   jax.ShapeDtypeStruct((B, WINDOW, SWA_KV_DIM), F32),
                   jax.ShapeDtypeStruct((B, WINDOW, SWA_KV_DIM), F32),
                   jax.ShapeDtypeStruct((B, CONV_HIST, CONV_DIM), F32)),
        grid=(B, nt),
        in_specs=[
            pl.BlockSpec(memory_space=pltpu.SMEM),
            pl.BlockSpec((1, TQ, D_MODEL), lambda b, t: (b, t, 0)),
            row(D_MODEL),
            _const_spec((D_MODEL, A_COLS)), _const_spec((D_MODEL, B_COLS)), _const_spec((D_MODEL, C_COLS)),
            _const_spec((CONV_WIDTH, CONV_DIM)), row(CONV_DIM), row(CONV_DIM), row(CONV_DIM),
            _const_spec((CONV_DIM, CONV_DIM)), row(CONV_DIM),
            _const_spec((2, SWA_KV_HEADS, SWA_PAIRS * QBLK, 2 * KEY_PAD)),
            pl.BlockSpec((1, MEM_TOKENS, MEM_DIM), lambda b, t: (b, 0, 0)),
            pl.BlockSpec((1, MEM_TOKENS, MEM_DIM), lambda b, t: (b, 0, 0)),
            _const_spec((D_MODEL, D_MODEL)), row(D_MODEL),
        ],
        out_specs=(pl.BlockSpec((1, TQ, D_MODEL), lambda b, t: (b, t, 0)),
                   pl.BlockSpec((1, WINDOW, SWA_KV_DIM), lambda b, t: (b, 0, 0)),
                   pl.BlockSpec((1, WINDOW, SWA_KV_DIM), lambda b, t: (b, 0, 0)),
                   pl.BlockSpec((1, CONV_HIST, CONV_DIM), lambda b, t: (b, 0, 0))),
        scratch_shapes=[
            pltpu.VMEM((TQ, D_MODEL), BF16),
            pltpu.VMEM((TQ, B_COLS), F32),
            pltpu.VMEM((TQ, D_MODEL), BF16),
            pltpu.VMEM((TQ, CONV_DIM), F32),
            pltpu.VMEM((HIST_ROWS + TQ, CONV_DIM), F32),
            pltpu.VMEM((4, WINDOW + TQ, 2 * SWA_HEAD_DIM), BF16),
            pltpu.VMEM((4, WINDOW + TQ, 2 * SWA_HEAD_DIM), BF16),
        ],
        compiler_params=pltpu.CompilerParams(dimension_semantics=("arbitrary", "arbitrary"),
                                             vmem_limit_bytes=VMEM_LIMIT),
        name="prompt_layer",
    )(sinks, x, ng, wa, wb, wc, cw, cb, lg, lb, wpw, bpw, bias, mk, mv, wo, fg)


def _sample_kernel(sinks_ref, x_ref, ng_ref, wa_ref, wb_ref, wc_ref, cw_ref, cb_ref, lg_ref, lb_ref, wpw_ref,
                   bpw_ref, bias_ref, ck_ref, cv_ref, cc_ref, mk_ref, mv_ref, wo_ref, fg_ref,
                   out_ref, nk_ref, nv_ref, nc_ref,
                   h_s, z_s, y_s, c_s, u_s, k4_s, v4_s, *, final, n_seq, seq):
    rows = n_seq * seq
    x = x_ref[...]
    h_s[...] = _rms(x, ng_ref[...]).astype(BF16)

    z_s[:, 0:A_COLS] = jnp.dot(h_s[...], wa_ref[...], preferred_element_type=F32)
    off0 = HIST_ROWS - CONV_HIST
    u_s[0:off0, :] = jnp.zeros((off0, CONV_DIM), F32)
    for b in range(n_seq):
        r0 = b * seq
        u_s[off0:HIST_ROWS, :] = cc_ref[b]
        u_s[HIST_ROWS:HIST_ROWS + seq, :] = (z_s[r0:r0 + seq, 0:CONV_DIM]
                                             * _sigmoid(z_s[r0:r0 + seq, CONV_DIM:2 * CONV_DIM]))
        nc_ref[b] = u_s[HIST_ROWS + seq - CONV_HIST:HIST_ROWS + seq, :]
        acc = jnp.zeros((seq, CONV_DIM), F32) + cb_ref[...]
        for k in range(CONV_WIDTH):
            acc = acc + cw_ref[k:k + 1, :] * u_s[off0 + k:off0 + k + seq, :]
        c_s[r0:r0 + seq, :] = acc
    ya = _conv_post(c_s[...], lg_ref[...], lb_ref[...], wpw_ref[...], bpw_ref[...],
                    z_s[:, 2 * CONV_DIM:3 * CONV_DIM])
    y_s[:, 0:CONV_DIM] = ya.astype(BF16)

    z_s[:, 0:B_COLS] = jnp.dot(h_s[...], wb_ref[...], preferred_element_type=F32)
    n_keys = WINDOW + seq
    k4_s[:, n_keys:, :] = jnp.zeros((4, KEY_PAD - n_keys, 2 * SWA_HEAD_DIM), BF16)
    v4_s[:, n_keys:, :] = jnp.zeros((4, KEY_PAD - n_keys, 2 * SWA_HEAD_DIM), BF16)
    for b in range(n_seq):
        r0 = b * seq
        kf = jnp.concatenate([ck_ref[b], z_s[r0:r0 + seq, B_K0:B_K0 + SWA_KV_DIM]], axis=0)
        vf = jnp.concatenate([cv_ref[b], z_s[r0:r0 + seq, B_V0:B_V0 + SWA_KV_DIM]], axis=0)
        nk_ref[b] = kf[seq:, :]
        nv_ref[b] = vf[seq:, :]
        for i, part in enumerate(_split_halves(kf)):
            k4_s[i, 0:n_keys, :] = part
        for i, part in enumerate(_split_halves(vf)):
            v4_s[i, 0:n_keys, :] = part
        for kv in range(SWA_KV_HEADS):
            c0 = kv * SWA_PAIRS * 2 * SWA_HEAD_DIM
            lhs4 = jnp.concatenate(
                [(z_s[r0:r0 + seq, c0 + p * 128:c0 + (p + 1) * 128] * SWA_SCALE).astype(BF16)
                 for p in range(SWA_PAIRS)], axis=0)
            if kv == 0:
                kp = jnp.concatenate([k4_s[0], k4_s[3]], axis=0)
                vc = jnp.concatenate([v4_s[0], v4_s[3]], axis=0)
            else:
                kp = jnp.concatenate([k4_s[2], k4_s[1]], axis=0)
                vc = jnp.concatenate([v4_s[2], v4_s[1]], axis=0)
            sink_fn = lambda p, par, kv=kv: sinks_ref[kv * 2 * SWA_PAIRS + 2 * p + par]
            o = _swa_attend(lhs4, kp, vc, bias_ref[0, kv], sink_fn, seq)
            for p in range(SWA_PAIRS):
                gate = z_s[r0:r0 + seq, B_G0 + c0 + p * 128:B_G0 + c0 + (p + 1) * 128]
                yb = o[p * seq:(p + 1) * seq, :] * _silu(gate)
                y_s[r0:r0 + seq, CONV_DIM + c0 + p * 128:CONV_DIM + c0 + (p + 1) * 128] = yb.astype(BF16)

    z_s[:, 0:C_COLS] = jnp.dot(h_s[...], wc_ref[...], preferred_element_type=F32)
    for b in range(n_seq):
        r0 = b * seq
        yc = _mem_attend(z_s[r0:r0 + seq, 0:MEM_DIM], mk_ref[b].astype(BF16), mv_ref[b].astype(BF16))
        y_s[r0:r0 + seq, CONV_DIM + SWA_DIM:] = (yc * _silu(z_s[r0:r0 + seq, MEM_DIM:C_COLS])).astype(BF16)

    o = x + jnp.dot(y_s[...], wo_ref[...], preferred_element_type=F32)
    if final:
        o = _rms(o, fg_ref[...])
    out_ref[...] = o


def _sample_layer(x2d, sinks, ng, wa, wb, wc, cw, cb, lg, lb, wpw, bpw, bias, ck, cv, cc, mk, mv, wo, fg,
                  final, n_seq, seq):
    rows = n_seq * seq
    row = lambda n: _const_spec((1, n))
    full = lambda shape: _const_spec(shape)
    return pl.pallas_call(
        functools.partial(_sample_kernel, final=final, n_seq=n_seq, seq=seq),
        out_shape=(jax.ShapeDtypeStruct((rows, D_MODEL), F32),
                   jax.ShapeDtypeStruct((n_seq, WINDOW, SWA_KV_DIM), F32),
                   jax.ShapeDtypeStruct((n_seq, WINDOW, SWA_KV_DIM), F32),
                   jax.ShapeDtypeStruct((n_seq, CONV_HIST, CONV_DIM), F32)),
        grid=(1,),
        in_specs=[
            pl.BlockSpec(memory_space=pltpu.SMEM),
            full((rows, D_MODEL)),
            row(D_MODEL),
            full((D_MODEL, A_COLS)), full((D_MODEL, B_COLS)), full((D_MODEL, C_COLS)),
            full((CONV_WIDTH, CONV_DIM)), row(CONV_DIM), row(CONV_DIM), row(CONV_DIM),
            full((CONV_DIM, CONV_DIM)), row(CONV_DIM),
            full((1, SWA_KV_HEADS, SWA_PAIRS * seq, 2 * KEY_PAD)),
            full((n_seq, WINDOW, SWA_KV_DIM)), full((n_seq, WINDOW, SWA_KV_DIM)),
            full((n_seq, CONV_HIST, CONV_DIM)),
            full((n_seq, MEM_TOKENS, MEM_DIM)), full((n_seq, MEM_TOKENS, MEM_DIM)),
            full((D_MODEL, D_MODEL)), row(D_MODEL),
        ],
        out_specs=(pl.BlockSpec((rows, D_MODEL), lambda i: (0, 0)),
                   pl.BlockSpec((n_seq, WINDOW, SWA_KV_DIM), lambda i: (0, 0, 0)),
                   pl.BlockSpec((n_seq, WINDOW, SWA_KV_DIM), lambda i: (0, 0, 0)),
                   pl.BlockSpec((n_seq, CONV_HIST, CONV_DIM), lambda i: (0, 0, 0))),
        scratch_shapes=[
            pltpu.VMEM((rows, D_MODEL), BF16),
            pltpu.VMEM((rows, B_COLS), F32),
            pltpu.VMEM((rows, D_MODEL), BF16),
            pltpu.VMEM((rows, CONV_DIM), F32),
            pltpu.VMEM((HIST_ROWS + seq, CONV_DIM), F32),
            pltpu.VMEM((4, KEY_PAD, 2 * SWA_HEAD_DIM), BF16),
            pltpu.VMEM((4, KEY_PAD, 2 * SWA_HEAD_DIM), BF16),
        ],
        compiler_params=pltpu.CompilerParams(dimension_semantics=("arbitrary",), vmem_limit_bytes=VMEM_LIMIT),
        name="sample_layer",
    )(sinks, x2d, ng, wa, wb, wc, cw, cb, lg, lb, wpw, bpw, bias, ck, cv, cc, mk, mv, wo, fg)


def kernel(x_prompt, x_sample, mem_prompt, cache_swa_k, cache_swa_v, cache_conv, cache_mem_k, cache_mem_v, norm_g,
           w_in, conv_w, conv_b, conv_ln_g, conv_ln_b, w_pw, b_pw, swa_sinks, mem_norm_g, w_mem_kv, rel_bias,
           w_out, final_norm_g):
    B, T, _ = x_prompt.shape
    n_seq, seq, _ = x_sample.shape
    L = cache_swa_k.shape[2]
    assert T % TQ == 0 and L == WINDOW and seq % 16 == 0 and WINDOW + seq <= KEY_PAD

    wa = w_in[:, :, 0:A_COLS].astype(BF16)
    wb = w_in[:, :, A_COLS:A_COLS + B_COLS].astype(BF16)
    wc = w_in[:, :, A_COLS + B_COLS:].astype(BF16)
    wpw = w_pw.astype(BF16)
    wo = w_out.astype(BF16)
    wmem = w_mem_kv.astype(BF16)

    bias_p = _bias_tables(rel_bias, QBLK, _prompt_valid())
    bias_s = _bias_tables(rel_bias, seq, _sample_valid(seq, L + seq))

    mk_all, mv_all = _memory_kv(mem_prompt.reshape(B * MEM_TOKENS, D_MODEL), mem_norm_g, wmem)
    mk_all = mk_all.reshape(DEPTH, B, MEM_TOKENS, MEM_DIM)
    mv_all = mv_all.reshape(DEPTH, B, MEM_TOKENS, MEM_DIM)

    ck = cache_swa_k.reshape(DEPTH, n_seq, L, SWA_KV_DIM)
    cv = cache_swa_v.reshape(DEPTH, n_seq, L, SWA_KV_DIM)
    cmk = cache_mem_k.reshape(DEPTH, n_seq, MEM_TOKENS, MEM_DIM)
    cmv = cache_mem_v.reshape(DEPTH, n_seq, MEM_TOKENS, MEM_DIM)
    fg = final_norm_g.reshape(1, D_MODEL)

    hp = x_prompt
    hs = x_sample.reshape(n_seq * seq, D_MODEL)
    p_k, p_v, p_c, s_k, s_v, s_c = [], [], [], [], [], []
    for l in range(DEPTH):
        final = l == DEPTH - 1
        shared = (norm_g[l].reshape(1, D_MODEL), wa[l], wb[l], wc[l], conv_w[l], conv_b[l].reshape(1, CONV_DIM),
                  conv_ln_g[l].reshape(1, CONV_DIM), conv_ln_b[l].reshape(1, CONV_DIM), wpw[l],
                  b_pw[l].reshape(1, CONV_DIM))
        hp, nk, nv, nc = _prompt_layer(hp, swa_sinks[l], *shared, bias_p, mk_all[l], mv_all[l], wo[l], fg, final)
        p_k.append(nk)
        p_v.append(nv)
        p_c.append(nc)
        hs, nk, nv, nc = _sample_layer(hs, swa_sinks[l], *shared, bias_s, ck[l], cv[l], cache_conv[l], cmk[l],
                                       cmv[l], wo[l], fg, final, n_seq, seq)
        s_k.append(nk)
        s_v.append(nv)
        s_c.append(nc)

    kv_shape = (DEPTH, -1, WINDOW, SWA_KV_HEADS, SWA_HEAD_DIM)
    mem_shape = (DEPTH, B, MEM_TOKENS, MEM_HEADS, MEM_HEAD_DIM)
    return (hp, hs.reshape(n_seq, seq, D_MODEL),
            jnp.stack(p_k).reshape(kv_shape), jnp.stack(p_v).reshape(kv_shape), jnp.stack(p_c),
            mk_all.reshape(mem_shape), mv_all.reshape(mem_shape),
            jnp.stack(s_k).reshape(kv_shape), jnp.stack(s_v).reshape(kv_shape), jnp.stack(s_c))
```

```python
import functools

import numpy as np
import jax
import jax.numpy as jnp
from jax import lax
from jax.experimental import pallas as pl
from jax.experimental.pallas import tpu as pltpu

F32 = jnp.float32
BF16 = jnp.bfloat16

D_MODEL = 2048
DEPTH = 4
CHUNK = 64
CONV_DIM = 512
CONV_WIDTH = 31
CONV_HIST = CONV_WIDTH - 1
SWA_HEAD_DIM = 64
SWA_DIM = 1024
SWA_HEADS = 16
SWA_KV_HEADS = 2
SWA_KV_DIM = SWA_KV_HEADS * SWA_HEAD_DIM
SWA_PAIRS = 4
WINDOW = 128
MEM_TOKENS = 256
MEM_HEADS = 4
MEM_DIM = 512
MEM_HEAD_DIM = 128
NUM_BUCKETS = 32
MAX_DISTANCE = 128
EPS = 1e-6
NEG_INF = -1e30
SWA_SCALE = SWA_HEAD_DIM ** -0.5
MEM_SCALE = MEM_HEAD_DIM ** -0.5

A_COLS = 3 * CONV_DIM
B_COLS = SWA_DIM + 2 * SWA_KV_DIM + SWA_DIM
C_COLS = 2 * MEM_DIM
IN_COLS = A_COLS + B_COLS + C_COLS
B_K0 = SWA_DIM
B_V0 = SWA_DIM + SWA_KV_DIM
B_G0 = SWA_DIM + 2 * SWA_KV_DIM
KEY_PAD = 256
QBLK = 128
HIST_ROWS = 32
TQ = 256
CONV_RB = 32
VMEM_LIMIT = 60 * 1024 * 1024

VEC_CONV_B = CONV_WIDTH
VEC_LN_G = CONV_WIDTH + 1
VEC_LN_B = CONV_WIDTH + 2
VEC_B_PW = CONV_WIDTH + 3
VEC_ROWS = 40

_NT = (((1,), (1,)), ((), ()))


def _sigmoid(x):
    return 1.0 / (1.0 + jnp.exp(-x))


def _silu(x):
    return x * _sigmoid(x)


def _rms(x, g):
    return x * lax.rsqrt(jnp.mean(x * x, axis=-1, keepdims=True) + EPS) * g


def _const_spec(shape):
    nd = len(shape)
    return pl.BlockSpec(shape, lambda *_: (0,) * nd, pipeline_mode=pl.Buffered(1))


def _layer_spec(shape, l):
    nd = len(shape)
    return pl.BlockSpec((None,) + tuple(shape), lambda *_: (l,) + (0,) * nd, pipeline_mode=pl.Buffered(1))


def _t5_bucket(rel):
    half = NUM_BUCKETS // 2
    exact = half // 2
    side = jnp.where(rel > 0, half, 0)
    n = jnp.abs(rel)
    nf = jnp.maximum(n, 1).astype(jnp.float32)
    large = exact + (jnp.log(nf / exact) / np.float32(np.log(MAX_DISTANCE / exact)) * (half - exact)).astype(jnp.int32)
    large = jnp.minimum(large, half - 1)
    return side + jnp.where(n < exact, n, large)


def _bias_kernel(table_ref, bucket_ref, valid_ref, out_ref):
    h = pl.program_id(0)
    bucket = bucket_ref[...]
    acc = jnp.zeros(bucket.shape, F32)
    for b in range(NUM_BUCKETS):
        acc = jnp.where(bucket == b, table_ref[b, h], acc)
    out_ref[0] = jnp.where(valid_ref[...] != 0, acc, NEG_INF)


def _bias_table(rel_bias, n_q, valid):
    rel = jnp.arange(KEY_PAD)[None, :] - WINDOW - jnp.arange(n_q)[:, None]
    bucket = _t5_bucket(rel).astype(jnp.int32)
    raw = pl.pallas_call(
        _bias_kernel,
        out_shape=jax.ShapeDtypeStruct((SWA_HEADS, n_q, KEY_PAD), F32),
        grid=(SWA_HEADS,),
        in_specs=[
            pl.BlockSpec(memory_space=pltpu.SMEM),
            pl.BlockSpec((n_q, KEY_PAD), lambda h: (0, 0)),
            pl.BlockSpec((n_q, KEY_PAD), lambda h: (0, 0)),
        ],
        out_specs=pl.BlockSpec((1, n_q, KEY_PAD), lambda h: (h, 0, 0)),
        name="bias_table",
    )(rel_bias, bucket, jnp.asarray(valid, jnp.int32))
    raw = raw.reshape(SWA_KV_HEADS, SWA_PAIRS, 2, n_q, KEY_PAD)
    return raw.transpose(0, 1, 3, 2, 4).reshape(SWA_KV_HEADS, SWA_PAIRS * n_q, 2 * KEY_PAD)


def _prompt_valid():
    i = np.arange(QBLK)[:, None]
    j = np.arange(KEY_PAD)[None, :]
    kc = j // CHUNK - WINDOW // CHUNK
    qc = i // CHUNK
    return ((kc <= qc) & (kc >= qc - WINDOW // CHUNK)).astype(np.int32)


def _sample_valid(n_q, n_keys):
    j = np.arange(KEY_PAD)[None, :]
    return np.broadcast_to(j < n_keys, (n_q, KEY_PAD)).astype(np.int32)


def _conv_post(c, vec_ref, wpw, gate):
    mu = jnp.mean(c, axis=-1, keepdims=True)
    xc = c - mu
    y = (xc * lax.rsqrt(jnp.mean(xc * xc, axis=-1, keepdims=True) + EPS) * vec_ref[VEC_LN_G:VEC_LN_G + 1, :]
         + vec_ref[VEC_LN_B:VEC_LN_B + 1, :])
    y = _silu(y)
    y = jnp.dot(y.astype(BF16), wpw, preferred_element_type=F32) + vec_ref[VEC_B_PW:VEC_B_PW + 1, :]
    return y * _silu(gate)


def _split_halves(x):
    xr = pltpu.roll(x, SWA_HEAD_DIM, axis=1)
    lo = lax.broadcasted_iota(jnp.int32, x.shape, 1) < SWA_HEAD_DIM
    zero = jnp.zeros_like(x)
    return (jnp.where(lo, x, zero).astype(BF16), jnp.where(lo, zero, x).astype(BF16),
            jnp.where(lo, xr, zero).astype(BF16), jnp.where(lo, zero, xr).astype(BF16))


def _swa_attend(lhs4, kp, vc, bias, sink_fn, nq, no_history=None):
    s = lax.dot_general(lhs4, kp, _NT, preferred_element_type=F32) + bias
    if no_history is not None:
        key = lax.broadcasted_iota(jnp.int32, (1, 2 * KEY_PAD), 1) % KEY_PAD
        s = jnp.where(key < jnp.where(no_history, WINDOW, 0), NEG_INF, s)
    lane_lo = lax.broadcasted_iota(jnp.int32, (nq, 2 * SWA_HEAD_DIM), 1) < SWA_HEAD_DIM
    prow, inv_rows = [], []
    for p in range(SWA_PAIRS):
        es, invs = [], []
        for par in range(2):
            sh = s[p * nq:(p + 1) * nq, par * KEY_PAD:(par + 1) * KEY_PAD]
            sink = sink_fn(p, par)
            m = jnp.maximum(jnp.max(sh, axis=1, keepdims=True), sink)
            e = jnp.exp(sh - m)
            den = jnp.sum(e, axis=1, keepdims=True) + jnp.exp(sink - m)
            es.append(e.astype(BF16))
            invs.append(1.0 / den)
        prow.append(jnp.concatenate(es, axis=1))
        inv_rows.append(jnp.where(lane_lo, invs[0], invs[1]))
    pm = jnp.concatenate(prow, axis=0)
    o = jnp.dot(pm, vc, preferred_element_type=F32)
    return o * jnp.concatenate(inv_rows, axis=0)


def _mem_attend(cq, mk, mv):
    outs = []
    for hh in range(MEM_HEADS):
        cs = slice(hh * MEM_HEAD_DIM, (hh + 1) * MEM_HEAD_DIM)
        s = lax.dot_general(cq[:, cs].astype(BF16), mk[:, cs], _NT, preferred_element_type=F32) * MEM_SCALE
        m = jnp.max(s, axis=1, keepdims=True)
        e = jnp.exp(s - m)
        p = e / jnp.sum(e, axis=1, keepdims=True)
        outs.append(jnp.dot(p.astype(BF16), mv[:, cs], preferred_element_type=F32))
    return jnp.concatenate(outs, axis=1)


def _mem_attend_interleaved(cq, mk, mv, head_of_row, head_of_col):
    r = cq.shape[0]
    q4 = jnp.concatenate([cq[:, hh * MEM_HEAD_DIM:(hh + 1) * MEM_HEAD_DIM] for hh in range(MEM_HEADS)],
                         axis=0).astype(BF16)
    s = lax.dot_general(q4, mk, _NT, preferred_element_type=F32) * MEM_SCALE
    s = jnp.where(head_of_row == head_of_col, s, NEG_INF)
    m = jnp.max(s, axis=1, keepdims=True)
    e = jnp.exp(s - m)
    p = e / jnp.sum(e, axis=1, keepdims=True)
    o = jnp.dot(p.astype(BF16), mv, preferred_element_type=F32)
    return jnp.concatenate([o[hh * r:(hh + 1) * r, :] for hh in range(MEM_HEADS)], axis=1)


def _memkv_kernel(mem_ref, g_ref, w_ref, mk_ref, mv_ref):
    h = _rms(mem_ref[...], g_ref[...]).astype(BF16)
    kv = jnp.dot(h, w_ref[...], preferred_element_type=F32)
    mk_ref[...] = kv[:, :MEM_DIM]
    mv_ref[...] = kv[:, MEM_DIM:]


def _memory_kv(mem2d, mem_norm_g3, w_mem_kv_bf):
    rows = mem2d.shape[0]
    return pl.pallas_call(
        _memkv_kernel,
        out_shape=(jax.ShapeDtypeStruct((DEPTH, rows, MEM_DIM), F32),
                   jax.ShapeDtypeStruct((DEPTH, rows, MEM_DIM), F32)),
        grid=(DEPTH,),
        in_specs=[
            pl.BlockSpec((rows, D_MODEL), lambda l: (0, 0)),
            pl.BlockSpec((None, 1, D_MODEL), lambda l: (l, 0, 0)),
            pl.BlockSpec((None, D_MODEL, 2 * MEM_DIM), lambda l: (l, 0, 0)),
        ],
        out_specs=(pl.BlockSpec((None, rows, MEM_DIM), lambda l: (l, 0, 0)),
                   pl.BlockSpec((None, rows, MEM_DIM), lambda l: (l, 0, 0))),
        compiler_params=pltpu.CompilerParams(dimension_semantics=("arbitrary",), vmem_limit_bytes=VMEM_LIMIT),
        name="memory_kv",
    )(mem2d, mem_norm_g3, w_mem_kv_bf)


def _prompt_kernel(sinks_ref, x_ref, ng_ref, win_ref, vec_ref, wpw_ref, bias_ref, mk_ref, mv_ref, wo_ref, fg_ref,
                   out_ref, nk_ref, nv_ref, nc_ref,
                   h_s, za_s, zb_s, zc_s, y_s, c_s, u_s, k4_s, v4_s, *, layer, final):
    t = pl.program_id(1)

    @pl.when(t == 0)
    def _():
        u_s[0:HIST_ROWS, :] = jnp.zeros((HIST_ROWS, CONV_DIM), F32)
        k4_s[:, 0:WINDOW, :] = jnp.zeros((4, WINDOW, 2 * SWA_HEAD_DIM), BF16)
        v4_s[:, 0:WINDOW, :] = jnp.zeros((4, WINDOW, 2 * SWA_HEAD_DIM), BF16)

    x = x_ref[...]
    h_s[...] = _rms(x, ng_ref[...]).astype(BF16)
    za_s[...] = jnp.dot(h_s[...], win_ref[:, 0:A_COLS], preferred_element_type=F32)
    zb_s[...] = jnp.dot(h_s[...], win_ref[:, A_COLS:A_COLS + B_COLS], preferred_element_type=F32)
    zc_s[...] = jnp.dot(h_s[...], win_ref[:, A_COLS + B_COLS:IN_COLS], preferred_element_type=F32)

    u_s[HIST_ROWS:HIST_ROWS + TQ, :] = za_s[:, 0:CONV_DIM] * _sigmoid(za_s[:, CONV_DIM:2 * CONV_DIM])
    off0 = HIST_ROWS - CONV_HIST
    for rb in range(TQ // CONV_RB):
        r0 = rb * CONV_RB
        acc = jnp.zeros((CONV_RB, CONV_DIM), F32) + vec_ref[VEC_CONV_B:VEC_CONV_B + 1, :]
        for k in range(CONV_WIDTH):
            acc = acc + vec_ref[k:k + 1, :] * u_s[r0 + off0 + k:r0 + off0 + k + CONV_RB, :]
        c_s[r0:r0 + CONV_RB, :] = acc
    u_s[0:HIST_ROWS, :] = u_s[TQ:TQ + HIST_ROWS, :]
    ya = _conv_post(c_s[...], vec_ref, wpw_ref[...], za_s[:, 2 * CONV_DIM:3 * CONV_DIM])
    y_s[:, 0:CONV_DIM] = ya.astype(BF16)

    for i, part in enumerate(_split_halves(zb_s[:, B_K0:B_K0 + SWA_KV_DIM])):
        k4_s[i, WINDOW:WINDOW + TQ, :] = part
    for i, part in enumerate(_split_halves(zb_s[:, B_V0:B_V0 + SWA_KV_DIM])):
        v4_s[i, WINDOW:WINDOW + TQ, :] = part
    for blk in range(TQ // QBLK):
        q0 = blk * QBLK
        band = slice(q0, q0 + KEY_PAD)
        for kv in range(SWA_KV_HEADS):
            c0 = kv * SWA_PAIRS * 2 * SWA_HEAD_DIM
            lhs4 = jnp.concatenate(
                [(zb_s[q0:q0 + QBLK, c0 + p * 128:c0 + (p + 1) * 128] * SWA_SCALE).astype(BF16)
                 for p in range(SWA_PAIRS)], axis=0)
            if kv == 0:
                kp = jnp.concatenate([k4_s[0, band, :], k4_s[3, band, :]], axis=0)
                vc = jnp.concatenate([v4_s[0, band, :], v4_s[3, band, :]], axis=0)
            else:
                kp = jnp.concatenate([k4_s[2, band, :], k4_s[1, band, :]], axis=0)
                vc = jnp.concatenate([v4_s[2, band, :], v4_s[1, band, :]], axis=0)
            sink_fn = lambda p, par, kv=kv: sinks_ref[layer, kv * 2 * SWA_PAIRS + 2 * p + par]
            o = _swa_attend(lhs4, kp, vc, bias_ref[kv], sink_fn, QBLK, no_history=(t == 0) if blk == 0 else None)
            for p in range(SWA_PAIRS):
                gate = zb_s[q0:q0 + QBLK, B_G0 + c0 + p * 128:B_G0 + c0 + (p + 1) * 128]
                yb = o[p * QBLK:(p + 1) * QBLK, :] * _silu(gate)
                y_s[q0:q0 + QBLK, CONV_DIM + c0 + p * 128:CONV_DIM + c0 + (p + 1) * 128] = yb.astype(BF16)
    k4_s[:, 0:WINDOW, :] = k4_s[:, TQ:TQ + WINDOW, :]
    v4_s[:, 0:WINDOW, :] = v4_s[:, TQ:TQ + WINDOW, :]

    yc = _mem_attend(zc_s[:, 0:MEM_DIM], mk_ref[...].astype(BF16), mv_ref[...].astype(BF16))
    y_s[:, CONV_DIM + SWA_DIM:] = (yc * _silu(zc_s[:, MEM_DIM:C_COLS])).astype(BF16)

    o = x + jnp.dot(y_s[...], wo_ref[...], preferred_element_type=F32)
    if final:
        o = _rms(o, fg_ref[...])
    out_ref[...] = o

    @pl.when(t == pl.num_programs(1) - 1)
    def _():
        nc_ref[...] = u_s[HIST_ROWS - CONV_HIST:HIST_ROWS, :]
        nk_ref[...] = zb_s[TQ - WINDOW:, B_K0:B_K0 + SWA_KV_DIM]
        nv_ref[...] = zb_s[TQ - WINDOW:, B_V0:B_V0 + SWA_KV_DIM]


def _prompt_layer(x, layer, sinks, ng, win, vec, wpw, bias, mk, mv, wo, fg, final):
    B, T, _ = x.shape
    nt = T // TQ
    return pl.pallas_call(
        functools.partial(_prompt_kernel, layer=layer, final=final),
        out_shape=(jax.ShapeDtypeStruct((B, T, D_MODEL), F32),
                   jax.ShapeDtypeStruct((B, WINDOW, SWA_KV_DIM), F32),
                   jax.ShapeDtypeStruct((B, WINDOW, SWA_KV_DIM), F32),
                   jax.ShapeDtypeStruct((B, CONV_HIST, CONV_DIM), F32)),
        grid=(B, nt),
        in_specs=[
            pl.BlockSpec(memory_space=pltpu.SMEM),
            pl.BlockSpec((None, TQ, D_MODEL), lambda b, t: (b, t, 0)),
            _layer_spec((1, D_MODEL), layer),
            _layer_spec((D_MODEL, IN_COLS), layer),
            _layer_spec((VEC_ROWS, CONV_DIM), layer),
            _layer_spec((CONV_DIM, CONV_DIM), layer),
            _const_spec((SWA_KV_HEADS, SWA_PAIRS * QBLK, 2 * KEY_PAD)),
            pl.BlockSpec((None, MEM_TOKENS, MEM_DIM), lambda b, t: (layer, b, 0)),
            pl.BlockSpec((None, MEM_TOKENS, MEM_DIM), lambda b, t: (layer, b, 0)),
            _layer_spec((D_MODEL, D_MODEL), layer),
            _const_spec((1, D_MODEL)),
        ],
        out_specs=(pl.BlockSpec((None, TQ, D_MODEL), lambda b, t: (b, t, 0)),
                   pl.BlockSpec((None, WINDOW, SWA_KV_DIM), lambda b, t: (b, 0, 0)),
                   pl.BlockSpec((None, WINDOW, SWA_KV_DIM), lambda b, t: (b, 0, 0)),
                   pl.BlockSpec((None, CONV_HIST, CONV_DIM), lambda b, t: (b, 0, 0))),
        scratch_shapes=[
            pltpu.VMEM((TQ, D_MODEL), BF16),
            pltpu.VMEM((TQ, A_COLS), F32),
            pltpu.VMEM((TQ, B_COLS), F32),
            pltpu.VMEM((TQ, C_COLS), F32),
            pltpu.VMEM((TQ, D_MODEL), BF16),
            pltpu.VMEM((TQ, CONV_DIM), F32),
            pltpu.VMEM((HIST_ROWS + TQ, CONV_DIM), F32),
            pltpu.VMEM((4, WINDOW + TQ, 2 * SWA_HEAD_DIM), BF16),
            pltpu.VMEM((4, WINDOW + TQ, 2 * SWA_HEAD_DIM), BF16),
        ],
        compiler_params=pltpu.CompilerParams(dimension_semantics=("arbitrary", "arbitrary"),
                                             vmem_limit_bytes=VMEM_LIMIT),
        name="prompt_layer",
    )(sinks, x, ng, win, vec, wpw, bias, mk, mv, wo, fg)


def _sample_kernel(sinks_ref, x_ref, ng_ref, win_ref, vec_ref, wpw_ref, bias_ref, ck_ref, cv_ref, cc_ref, mk_ref,
                   mv_ref, wo_ref, fg_ref,
                   out_ref, nk_ref, nv_ref, nc_ref,
                   h_s, za_s, zb_s, zc_s, y_s, c_s, u_s, k4_s, v4_s, *, layer, final, n_seq, seq):
    x = x_ref[...]
    h_s[...] = _rms(x, ng_ref[...]).astype(BF16)
    za_s[...] = jnp.dot(h_s[...], win_ref[:, 0:A_COLS], preferred_element_type=F32)
    zb_s[...] = jnp.dot(h_s[...], win_ref[:, A_COLS:A_COLS + B_COLS], preferred_element_type=F32)
    zc_s[...] = jnp.dot(h_s[...], win_ref[:, A_COLS + B_COLS:IN_COLS], preferred_element_type=F32)

    off0 = HIST_ROWS - CONV_HIST
    u_s[0:off0, :] = jnp.zeros((off0, CONV_DIM), F32)
    for b in range(n_seq):
        r0 = b * seq
        u_s[off0:HIST_ROWS, :] = cc_ref[b]
        u_s[HIST_ROWS:HIST_ROWS + seq, :] = (za_s[r0:r0 + seq, 0:CONV_DIM]
                                             * _sigmoid(za_s[r0:r0 + seq, CONV_DIM:2 * CONV_DIM]))
        nc_ref[b] = u_s[HIST_ROWS + seq - CONV_HIST:HIST_ROWS + seq, :]
        acc = jnp.zeros((seq, CONV_DIM), F32) + vec_ref[VEC_CONV_B:VEC_CONV_B + 1, :]
        for k in range(CONV_WIDTH):
            acc = acc + vec_ref[k:k + 1, :] * u_s[off0 + k:off0 + k + seq, :]
        c_s[r0:r0 + seq, :] = acc
    ya = _conv_post(c_s[...], vec_ref, wpw_ref[...], za_s[:, 2 * CONV_DIM:3 * CONV_DIM])
    y_s[:, 0:CONV_DIM] = ya.astype(BF16)

    n_keys = WINDOW + seq
    k4_s[:, n_keys:, :] = jnp.zeros((4, KEY_PAD - n_keys, 2 * SWA_HEAD_DIM), BF16)
    v4_s[:, n_keys:, :] = jnp.zeros((4, KEY_PAD - n_keys, 2 * SWA_HEAD_DIM), BF16)
    for b in range(n_seq):
        r0 = b * seq
        kf = jnp.concatenate([ck_ref[b], zb_s[r0:r0 + seq, B_K0:B_K0 + SWA_KV_DIM]], axis=0)
        vf = jnp.concatenate([cv_ref[b], zb_s[r0:r0 + seq, B_V0:B_V0 + SWA_KV_DIM]], axis=0)
        nk_ref[b] = kf[seq:, :]
        nv_ref[b] = vf[seq:, :]
        for i, part in enumerate(_split_halves(kf)):
            k4_s[i, 0:n_keys, :] = part
        for i, part in enumerate(_split_halves(vf)):
            v4_s[i, 0:n_keys, :] = part
        for kv in range(SWA_KV_HEADS):
            c0 = kv * SWA_PAIRS * 2 * SWA_HEAD_DIM
            lhs4 = jnp.concatenate(
                [(zb_s[r0:r0 + seq, c0 + p * 128:c0 + (p + 1) * 128] * SWA_SCALE).astype(BF16)
                 for p in range(SWA_PAIRS)], axis=0)
            if kv == 0:
                kp = jnp.concatenate([k4_s[0], k4_s[3]], axis=0)
                vc = jnp.concatenate([v4_s[0], v4_s[3]], axis=0)
            else:
                kp = jnp.concatenate([k4_s[2], k4_s[1]], axis=0)
                vc = jnp.concatenate([v4_s[2], v4_s[1]], axis=0)
            sink_fn = lambda p, par, kv=kv: sinks_ref[layer, kv * 2 * SWA_PAIRS + 2 * p + par]
            o = _swa_attend(lhs4, kp, vc, bias_ref[kv], sink_fn, seq)
            for p in range(SWA_PAIRS):
                gate = zb_s[r0:r0 + seq, B_G0 + c0 + p * 128:B_G0 + c0 + (p + 1) * 128]
                yb = o[p * seq:(p + 1) * seq, :] * _silu(gate)
                y_s[r0:r0 + seq, CONV_DIM + c0 + p * 128:CONV_DIM + c0 + (p + 1) * 128] = yb.astype(BF16)

    n_mem_rows = MEM_TOKENS * MEM_HEADS
    head_of_row = lax.broadcasted_iota(jnp.int32, (MEM_HEADS * seq, n_mem_rows), 0) // seq
    head_of_col = lax.broadcasted_iota(jnp.int32, (MEM_HEADS * seq, n_mem_rows), 1) % MEM_HEADS
    for b in range(n_seq):
        r0 = b * seq
        yc = _mem_attend_interleaved(zc_s[r0:r0 + seq, 0:MEM_DIM], mk_ref[b].astype(BF16), mv_ref[b].astype(BF16),
                                     head_of_row, head_of_col)
        y_s[r0:r0 + seq, CONV_DIM + SWA_DIM:] = (yc * _silu(zc_s[r0:r0 + seq, MEM_DIM:C_COLS])).astype(BF16)

    o = x + jnp.dot(y_s[...], wo_ref[...], preferred_element_type=F32)
    if final:
        o = _rms(o, fg_ref[...])
    out_ref[...] = o


def _sample_layer(x2d, layer, sinks, ng, win, vec, wpw, bias, ck, cv, cc, mk, mv, wo, fg, final, n_seq, seq):
    rows = n_seq * seq
    n_mem_rows = MEM_TOKENS * MEM_HEADS
    return pl.pallas_call(
        functools.partial(_sample_kernel, layer=layer, final=final, n_seq=n_seq, seq=seq),
        out_shape=(jax.ShapeDtypeStruct((rows, D_MODEL), F32),
                   jax.ShapeDtypeStruct((n_seq, WINDOW, SWA_KV_DIM), F32),
                   jax.ShapeDtypeStruct((n_seq, WINDOW, SWA_KV_DIM), F32),
                   jax.ShapeDtypeStruct((n_seq, CONV_HIST, CONV_DIM), F32)),
        grid=(1,),
        in_specs=[
            pl.BlockSpec(memory_space=pltpu.SMEM),
            _const_spec((rows, D_MODEL)),
            _layer_spec((1, D_MODEL), layer),
            _layer_spec((D_MODEL, IN_COLS), layer),
            _layer_spec((VEC_ROWS, CONV_DIM), layer),
            _layer_spec((CONV_DIM, CONV_DIM), layer),
            _const_spec((SWA_KV_HEADS, SWA_PAIRS * seq, 2 * KEY_PAD)),
            _layer_spec((n_seq, WINDOW, SWA_KV_DIM), layer),
            _layer_spec((n_seq, WINDOW, SWA_KV_DIM), layer),
            _layer_spec((n_seq, CONV_HIST, CONV_DIM), layer),
            _layer_spec((n_seq, n_mem_rows, MEM_HEAD_DIM), layer),
            _layer_spec((n_seq, n_mem_rows, MEM_HEAD_DIM), layer),
            _layer_spec((D_MODEL, D_MODEL), layer),
            _const_spec((1, D_MODEL)),
        ],
        out_specs=(pl.BlockSpec((rows, D_MODEL), lambda i: (0, 0)),
                   pl.BlockSpec((n_seq, WINDOW, SWA_KV_DIM), lambda i: (0, 0, 0)),
                   pl.BlockSpec((n_seq, WINDOW, SWA_KV_DIM), lambda i: (0, 0, 0)),
                   pl.BlockSpec((n_seq, CONV_HIST, CONV_DIM), lambda i: (0, 0, 0))),
        scratch_shapes=[
            pltpu.VMEM((rows, D_MODEL), BF16),
            pltpu.VMEM((rows, A_COLS), F32),
            pltpu.VMEM((rows, B_COLS), F32),
            pltpu.VMEM((rows, C_COLS), F32),
            pltpu.VMEM((rows, D_MODEL), BF16),
            pltpu.VMEM((rows, CONV_DIM), F32),
            pltpu.VMEM((HIST_ROWS + seq, CONV_DIM), F32),
            pltpu.VMEM((4, KEY_PAD, 2 * SWA_HEAD_DIM), BF16),
            pltpu.VMEM((4, KEY_PAD, 2 * SWA_HEAD_DIM), BF16),
        ],
        compiler_params=pltpu.CompilerParams(dimension_semantics=("arbitrary",), vmem_limit_bytes=VMEM_LIMIT),
        name="sample_layer",
    )(sinks, x2d, ng, win, vec, wpw, bias, ck, cv, cc, mk, mv, wo, fg)


def kernel(x_prompt, x_sample, mem_prompt, cache_swa_k, cache_swa_v, cache_conv, cache_mem_k, cache_mem_v, norm_g,
           w_in, conv_w, conv_b, conv_ln_g, conv_ln_b, w_pw, b_pw, swa_sinks, mem_norm_g, w_mem_kv, rel_bias,
           w_out, final_norm_g):
    B, T, _ = x_prompt.shape
    n_seq, seq, _ = x_sample.shape
    L = cache_swa_k.shape[2]
    assert T % TQ == 0 and L == WINDOW and seq % 16 == 0 and WINDOW + seq <= KEY_PAD

    win = w_in.astype(BF16)
    wpw = w_pw.astype(BF16)
    wo = w_out.astype(BF16)
    wmem = w_mem_kv.astype(BF16)
    vec = jnp.concatenate(
        [conv_w, conv_b[:, None, :], conv_ln_g[:, None, :], conv_ln_b[:, None, :], b_pw[:, None, :],
         jnp.zeros((DEPTH, VEC_ROWS - CONV_WIDTH - 4, CONV_DIM), F32)], axis=1)
    ng = norm_g.reshape(DEPTH, 1, D_MODEL)
    fg = final_norm_g.reshape(1, D_MODEL)

    bias_p = _bias_table(rel_bias, QBLK, _prompt_valid())
    bias_s = _bias_table(rel_bias, seq, _sample_valid(seq, L + seq))

    mk_all, mv_all = _memory_kv(mem_prompt.reshape(B * MEM_TOKENS, D_MODEL), mem_norm_g.reshape(DEPTH, 1, D_MODEL),
                                wmem)

    ck = cache_swa_k.reshape(DEPTH, n_seq, L, SWA_KV_DIM)
    cv = cache_swa_v.reshape(DEPTH, n_seq, L, SWA_KV_DIM)
    cmk = cache_mem_k.reshape(DEPTH, n_seq, MEM_TOKENS * MEM_HEADS, MEM_HEAD_DIM)
    cmv = cache_mem_v.reshape(DEPTH, n_seq, MEM_TOKENS * MEM_HEADS, MEM_HEAD_DIM)

    hp = x_prompt
    hs = x_sample.reshape(n_seq * seq, D_MODEL)
    p_k, p_v, p_c, s_k, s_v, s_c = [], [], [], [], [], []
    for l in range(DEPTH):
        final = l == DEPTH - 1
        hp, nk, nv, nc = _prompt_layer(hp, l, swa_sinks, ng, win, vec, wpw, bias_p, mk_all, mv_all, wo, fg, final)
        p_k.append(nk)
        p_v.append(nv)
        p_c.append(nc)
        hs, nk, nv, nc = _sample_layer(hs, l, swa_sinks, ng, win, vec, wpw, bias_s, ck, cv, cache_conv, cmk, cmv,
                                       wo, fg, final, n_seq, seq)
        s_k.append(nk)
        s_v.append(nv)
        s_c.append(nc)

    kv_shape = (DEPTH, -1, WINDOW, SWA_KV_HEADS, SWA_HEAD_DIM)
    mem_shape = (DEPTH, B, MEM_TOKENS, MEM_HEADS, MEM_HEAD_DIM)
    return (hp, hs.reshape(n_seq, seq, D_MODEL),
            jnp.stack(p_k).reshape(kv_shape), jnp.stack(p_v).reshape(kv_shape), jnp.stack(p_c),
            mk_all.reshape(mem_shape), mv_all.reshape(mem_shape),
            jnp.stack(s_k).reshape(kv_shape), jnp.stack(s_v).reshape(kv_shape), jnp.stack(s_c))
```

```python
import functools

import numpy as np
import jax
import jax.numpy as jnp
from jax import lax
from jax.experimental import pallas as pl
from jax.experimental.pallas import tpu as pltpu

F32 = jnp.float32
BF16 = jnp.bfloat16

D_MODEL = 2048
DEPTH = 4
CHUNK = 64
CONV_DIM = 512
CONV_WIDTH = 31
CONV_HIST = CONV_WIDTH - 1
SWA_HEAD_DIM = 64
SWA_DIM = 1024
SWA_HEADS = 16
SWA_KV_HEADS = 2
SWA_KV_DIM = SWA_KV_HEADS * SWA_HEAD_DIM
SWA_PAIRS = 4
WINDOW = 128
MEM_TOKENS = 256
MEM_HEADS = 4
MEM_DIM = 512
MEM_HEAD_DIM = 128
NUM_BUCKETS = 32
MAX_DISTANCE = 128
EPS = 1e-6
NEG_INF = -1e30
SWA_SCALE = SWA_HEAD_DIM ** -0.5
MEM_SCALE = MEM_HEAD_DIM ** -0.5

A_COLS = 3 * CONV_DIM
B_COLS = SWA_DIM + 2 * SWA_KV_DIM + SWA_DIM
C_COLS = 2 * MEM_DIM
IN_COLS = A_COLS + B_COLS + C_COLS
B_K0 = SWA_DIM
B_V0 = SWA_DIM + SWA_KV_DIM
B_G0 = SWA_DIM + 2 * SWA_KV_DIM
KEY_PAD = 256
QBLK = 128
SUBLANES = 8
HIST_ROWS = 32
TQ = 256
CONV_RB = 32
ROW_GROUPS = TQ // SUBLANES
VMEM_LIMIT = 60 * 1024 * 1024

VEC_CONV_B = CONV_WIDTH
VEC_LN_G = CONV_WIDTH + 1
VEC_LN_B = CONV_WIDTH + 2
VEC_B_PW = CONV_WIDTH + 3
VEC_ROWS = 40

_NT = (((1,), (1,)), ((), ()))


def _sigmoid(x):
    return 1.0 / (1.0 + jnp.exp(-x))


def _silu(x):
    return x * _sigmoid(x)


def _rms(x, g):
    return x * lax.rsqrt(jnp.mean(x * x, axis=-1, keepdims=True) + EPS) * g


def _const_spec(shape):
    nd = len(shape)
    return pl.BlockSpec(shape, lambda *_: (0,) * nd, pipeline_mode=pl.Buffered(1))


def _layer_spec(shape, l):
    nd = len(shape)
    return pl.BlockSpec((None,) + tuple(shape), lambda *_: (l,) + (0,) * nd, pipeline_mode=pl.Buffered(1))


def _t5_bucket(rel):
    half = NUM_BUCKETS // 2
    exact = half // 2
    side = jnp.where(rel > 0, half, 0)
    n = jnp.abs(rel)
    nf = jnp.maximum(n, 1).astype(jnp.float32)
    large = exact + (jnp.log(nf / exact) / np.float32(np.log(MAX_DISTANCE / exact)) * (half - exact)).astype(jnp.int32)
    large = jnp.minimum(large, half - 1)
    return side + jnp.where(n < exact, n, large)


def _bias_kernel(table_ref, bucket_ref, valid_ref, out_ref):
    h = pl.program_id(0)
    bucket = bucket_ref[...]
    acc = jnp.zeros(bucket.shape, F32)
    for b in range(NUM_BUCKETS):
        acc = jnp.where(bucket == b, table_ref[b, h], acc)
    out_ref[0] = jnp.where(valid_ref[...] != 0, acc, NEG_INF)


def _bias_table(rel_bias, n_q, valid):
    rel = jnp.arange(KEY_PAD)[None, :] - WINDOW - jnp.arange(n_q)[:, None]
    bucket = _t5_bucket(rel).astype(jnp.int32)
    raw = pl.pallas_call(
        _bias_kernel,
        out_shape=jax.ShapeDtypeStruct((SWA_HEADS, n_q, KEY_PAD), F32),
        grid=(SWA_HEADS,),
        in_specs=[
            pl.BlockSpec(memory_space=pltpu.SMEM),
            pl.BlockSpec((n_q, KEY_PAD), lambda h: (0, 0)),
            pl.BlockSpec((n_q, KEY_PAD), lambda h: (0, 0)),
        ],
        out_specs=pl.BlockSpec((1, n_q, KEY_PAD), lambda h: (h, 0, 0)),
        name="bias_table",
    )(rel_bias, bucket, jnp.asarray(valid, jnp.int32))
    raw = raw.reshape(SWA_KV_HEADS, SWA_PAIRS, 2, n_q, KEY_PAD)
    return raw.transpose(0, 1, 3, 2, 4).reshape(SWA_KV_HEADS, SWA_PAIRS * n_q, 2 * KEY_PAD)


def _regroup_matrix():
    r = np.arange(TQ)
    p = np.zeros((TQ, TQ), np.float32)
    p[r, ROW_GROUPS * (r % SUBLANES) + r // SUBLANES] = 1.0
    return p


def _prompt_valid():
    i = np.arange(QBLK)[:, None]
    j = np.arange(KEY_PAD)[None, :]
    kc = j // CHUNK - WINDOW // CHUNK
    qc = i // CHUNK
    return ((kc <= qc) & (kc >= qc - WINDOW // CHUNK)).astype(np.int32)


def _sample_valid(n_q, n_keys):
    j = np.arange(KEY_PAD)[None, :]
    return np.broadcast_to(j < n_keys, (n_q, KEY_PAD)).astype(np.int32)


def _conv_post(c, vec_ref, wpw, gate, unperm=None):
    mu = jnp.mean(c, axis=-1, keepdims=True)
    xc = c - mu
    y = (xc * lax.rsqrt(jnp.mean(xc * xc, axis=-1, keepdims=True) + EPS) * vec_ref[VEC_LN_G:VEC_LN_G + 1, :]
         + vec_ref[VEC_LN_B:VEC_LN_B + 1, :])
    y = _silu(y).astype(BF16)
    if unperm is not None:
        y = jnp.dot(unperm, y, preferred_element_type=F32).astype(BF16)
    y = jnp.dot(y, wpw, preferred_element_type=F32) + vec_ref[VEC_B_PW:VEC_B_PW + 1, :]
    return y * _silu(gate)


def _split_halves(x):
    xr = pltpu.roll(x, SWA_HEAD_DIM, axis=1)
    lo = lax.broadcasted_iota(jnp.int32, x.shape, 1) < SWA_HEAD_DIM
    zero = jnp.zeros_like(x)
    return (jnp.where(lo, x, zero).astype(BF16), jnp.where(lo, zero, x).astype(BF16),
            jnp.where(lo, xr, zero).astype(BF16), jnp.where(lo, zero, xr).astype(BF16))


def _swa_attend(lhs4, kp, vc, bias, sink_fn, nq, no_history=None):
    s = lax.dot_general(lhs4, kp, _NT, preferred_element_type=F32) + bias
    if no_history is not None:
        key = lax.broadcasted_iota(jnp.int32, (1, 2 * KEY_PAD), 1) % KEY_PAD
        s = jnp.where(key < jnp.where(no_history, WINDOW, 0), NEG_INF, s)
    lane_lo = lax.broadcasted_iota(jnp.int32, (nq, 2 * SWA_HEAD_DIM), 1) < SWA_HEAD_DIM
    prow, inv_rows = [], []
    for p in range(SWA_PAIRS):
        es, invs = [], []
        for par in range(2):
            sh = s[p * nq:(p + 1) * nq, par * KEY_PAD:(par + 1) * KEY_PAD]
            sink = sink_fn(p, par)
            m = jnp.maximum(jnp.max(sh, axis=1, keepdims=True), sink)
            e = jnp.exp(sh - m)
            den = jnp.sum(e, axis=1, keepdims=True) + jnp.exp(sink - m)
            es.append(e.astype(BF16))
            invs.append(1.0 / den)
        prow.append(jnp.concatenate(es, axis=1))
        inv_rows.append(jnp.where(lane_lo, invs[0], invs[1]))
    pm = jnp.concatenate(prow, axis=0)
    o = jnp.dot(pm, vc, preferred_element_type=F32)
    return o * jnp.concatenate(inv_rows, axis=0)


def _mem_attend(cq, mk, mv):
    outs = []
    for hh in range(MEM_HEADS):
        cs = slice(hh * MEM_HEAD_DIM, (hh + 1) * MEM_HEAD_DIM)
        s = lax.dot_general(cq[:, cs].astype(BF16), mk[:, cs], _NT, preferred_element_type=F32) * MEM_SCALE
        m = jnp.max(s, axis=1, keepdims=True)
        e = jnp.exp(s - m)
        p = e / jnp.sum(e, axis=1, keepdims=True)
        outs.append(jnp.dot(p.astype(BF16), mv[:, cs], preferred_element_type=F32))
    return jnp.concatenate(outs, axis=1)


def _mem_attend_interleaved(cq, mk, mv, head_of_row, head_of_col):
    r = cq.shape[0]
    q4 = jnp.concatenate([cq[:, hh * MEM_HEAD_DIM:(hh + 1) * MEM_HEAD_DIM] for hh in range(MEM_HEADS)],
                         axis=0).astype(BF16)
    s = lax.dot_general(q4, mk, _NT, preferred_element_type=F32) * MEM_SCALE
    s = jnp.where(head_of_row == head_of_col, s, NEG_INF)
    m = jnp.max(s, axis=1, keepdims=True)
    e = jnp.exp(s - m)
    p = e / jnp.sum(e, axis=1, keepdims=True)
    o = jnp.dot(p.astype(BF16), mv, preferred_element_type=F32)
    return jnp.concatenate([o[hh * r:(hh + 1) * r, :] for hh in range(MEM_HEADS)], axis=1)


def _memkv_kernel(mem_ref, g_ref, w_ref, mk_ref, mv_ref):
    h = _rms(mem_ref[...], g_ref[...]).astype(BF16)
    kv = jnp.dot(h, w_ref[...], preferred_element_type=F32)
    mk_ref[...] = kv[:, :MEM_DIM]
    mv_ref[...] = kv[:, MEM_DIM:]


def _memory_kv(mem2d, mem_norm_g3, w_mem_kv_bf):
    rows = mem2d.shape[0]
    return pl.pallas_call(
        _memkv_kernel,
        out_shape=(jax.ShapeDtypeStruct((DEPTH, rows, MEM_DIM), F32),
                   jax.ShapeDtypeStruct((DEPTH, rows, MEM_DIM), F32)),
        grid=(DEPTH,),
        in_specs=[
            pl.BlockSpec((rows, D_MODEL), lambda l: (0, 0)),
            pl.BlockSpec((None, 1, D_MODEL), lambda l: (l, 0, 0)),
            pl.BlockSpec((None, D_MODEL, 2 * MEM_DIM), lambda l: (l, 0, 0)),
        ],
        out_specs=(pl.BlockSpec((None, rows, MEM_DIM), lambda l: (l, 0, 0)),
                   pl.BlockSpec((None, rows, MEM_DIM), lambda l: (l, 0, 0))),
        compiler_params=pltpu.CompilerParams(dimension_semantics=("arbitrary",), vmem_limit_bytes=VMEM_LIMIT),
        name="memory_kv",
    )(mem2d, mem_norm_g3, w_mem_kv_bf)


def _prompt_kernel(sinks_ref, x_ref, ng_ref, win_ref, vec_ref, wpw_ref, perm_ref, unperm_ref, bias_ref, mk_ref,
                   mv_ref, wo_ref, fg_ref,
                   out_ref, nk_ref, nv_ref, nc_ref,
                   h_s, za_s, zb_s, zc_s, y_s, u_s, carry_s, k4_s, v4_s, *, layer, final):
    t = pl.program_id(1)

    @pl.when(t == 0)
    def _():
        carry_s[...] = jnp.zeros((TQ, CONV_DIM), F32)
        k4_s[:, 0:WINDOW, :] = jnp.zeros((4, WINDOW, 2 * SWA_HEAD_DIM), BF16)
        v4_s[:, 0:WINDOW, :] = jnp.zeros((4, WINDOW, 2 * SWA_HEAD_DIM), BF16)

    x = x_ref[...]
    h_s[...] = _rms(x, ng_ref[...]).astype(BF16)
    y_s[...] = jnp.dot(perm_ref[...], h_s[...], preferred_element_type=F32).astype(BF16)
    za_s[:, 0:2 * CONV_DIM] = jnp.dot(y_s[...], win_ref[:, 0:2 * CONV_DIM], preferred_element_type=F32)
    za_s[:, 2 * CONV_DIM:A_COLS] = jnp.dot(h_s[...], win_ref[:, 2 * CONV_DIM:A_COLS], preferred_element_type=F32)
    zb_s[...] = jnp.dot(h_s[...], win_ref[:, A_COLS:A_COLS + B_COLS], preferred_element_type=F32)
    zc_s[...] = jnp.dot(h_s[...], win_ref[:, A_COLS + B_COLS:IN_COLS], preferred_element_type=F32)

    u = za_s[:, 0:CONV_DIM] * _sigmoid(za_s[:, CONV_DIM:2 * CONV_DIM])
    u_s[TQ:2 * TQ, :] = u
    first_row = lax.broadcasted_iota(jnp.int32, (TQ, CONV_DIM), 0) % SUBLANES == 0
    u_s[0:TQ, :] = jnp.where(first_row, carry_s[...], pltpu.roll(u, 1, axis=0))
    carry_s[...] = pltpu.roll(u, TQ - (SUBLANES - 1), axis=0)
    for rb in range(TQ // CONV_RB):
        r0 = rb * CONV_RB
        acc = jnp.zeros((CONV_RB, CONV_DIM), F32) + vec_ref[VEC_CONV_B:VEC_CONV_B + 1, :]
        for k in range(CONV_WIDTH):
            src = r0 + SUBLANES * (k + ROW_GROUPS - CONV_HIST)
            acc = acc + vec_ref[k:k + 1, :] * u_s[src:src + CONV_RB, :]
        za_s[r0:r0 + CONV_RB, 0:CONV_DIM] = acc
    ya = _conv_post(za_s[:, 0:CONV_DIM], vec_ref, wpw_ref[...], za_s[:, 2 * CONV_DIM:3 * CONV_DIM], unperm_ref[...])
    y_s[:, 0:CONV_DIM] = ya.astype(BF16)

    for i, part in enumerate(_split_halves(zb_s[:, B_K0:B_K0 + SWA_KV_DIM])):
        k4_s[i, WINDOW:WINDOW + TQ, :] = part
    for i, part in enumerate(_split_halves(zb_s[:, B_V0:B_V0 + SWA_KV_DIM])):
        v4_s[i, WINDOW:WINDOW + TQ, :] = part
    for blk in range(TQ // QBLK):
        q0 = blk * QBLK
        band = slice(q0, q0 + KEY_PAD)
        for kv in range(SWA_KV_HEADS):
            c0 = kv * SWA_PAIRS * 2 * SWA_HEAD_DIM
            lhs4 = jnp.concatenate(
                [(zb_s[q0:q0 + QBLK, c0 + p * 128:c0 + (p + 1) * 128] * SWA_SCALE).astype(BF16)
                 for p in range(SWA_PAIRS)], axis=0)
            if kv == 0:
                kp = jnp.concatenate([k4_s[0, band, :], k4_s[3, band, :]], axis=0)
                vc = jnp.concatenate([v4_s[0, band, :], v4_s[3, band, :]], axis=0)
            else:
                kp = jnp.concatenate([k4_s[2, band, :], k4_s[1, band, :]], axis=0)
                vc = jnp.concatenate([v4_s[2, band, :], v4_s[1, band, :]], axis=0)
            sink_fn = lambda p, par, kv=kv: sinks_ref[layer, kv * 2 * SWA_PAIRS + 2 * p + par]
            o = _swa_attend(lhs4, kp, vc, bias_ref[kv], sink_fn, QBLK, no_history=(t == 0) if blk == 0 else None)
            for p in range(SWA_PAIRS):
                gate = zb_s[q0:q0 + QBLK, B_G0 + c0 + p * 128:B_G0 + c0 + (p + 1) * 128]
                yb = o[p * QBLK:(p + 1) * QBLK, :] * _silu(gate)
                y_s[q0:q0 + QBLK, CONV_DIM + c0 + p * 128:CONV_DIM + c0 + (p + 1) * 128] = yb.astype(BF16)
    k4_s[:, 0:WINDOW, :] = k4_s[:, TQ:TQ + WINDOW, :]
    v4_s[:, 0:WINDOW, :] = v4_s[:, TQ:TQ + WINDOW, :]

    yc = _mem_attend(zc_s[:, 0:MEM_DIM], mk_ref[...].astype(BF16), mv_ref[...].astype(BF16))
    y_s[:, CONV_DIM + SWA_DIM:] = (yc * _silu(zc_s[:, MEM_DIM:C_COLS])).astype(BF16)

    o = x + jnp.dot(y_s[...], wo_ref[...], preferred_element_type=F32)
    if final:
        o = _rms(o, fg_ref[...])
    out_ref[...] = o

    @pl.when(t == pl.num_programs(1) - 1)
    def _():
        tail = jnp.dot(h_s[TQ - HIST_ROWS:, :], win_ref[:, 0:2 * CONV_DIM], preferred_element_type=F32)
        nc_ref[...] = (tail[:, 0:CONV_DIM] * _sigmoid(tail[:, CONV_DIM:]))[HIST_ROWS - CONV_HIST:, :]
        nk_ref[...] = zb_s[TQ - WINDOW:, B_K0:B_K0 + SWA_KV_DIM]
        nv_ref[...] = zb_s[TQ - WINDOW:, B_V0:B_V0 + SWA_KV_DIM]


def _prompt_layer(x, layer, sinks, ng, win, vec, wpw, perm, unperm, bias, mk, mv, wo, fg, final):
    B, T, _ = x.shape
    nt = T // TQ
    return pl.pallas_call(
        functools.partial(_prompt_kernel, layer=layer, final=final),
        out_shape=(jax.ShapeDtypeStruct((B, T, D_MODEL), F32),
                   jax.ShapeDtypeStruct((B, WINDOW, SWA_KV_DIM), F32),
                   jax.ShapeDtypeStruct((B, WINDOW, SWA_KV_DIM), F32),
                   jax.ShapeDtypeStruct((B, CONV_HIST, CONV_DIM), F32)),
        grid=(B, nt),
        in_specs=[
            pl.BlockSpec(memory_space=pltpu.SMEM),
            pl.BlockSpec((None, TQ, D_MODEL), lambda b, t: (b, t, 0)),
            _layer_spec((1, D_MODEL), layer),
            _layer_spec((D_MODEL, IN_COLS), layer),
            _layer_spec((VEC_ROWS, CONV_DIM), layer),
            _layer_spec((CONV_DIM, CONV_DIM), layer),
            _const_spec((TQ, TQ)), _const_spec((TQ, TQ)),
            _const_spec((SWA_KV_HEADS, SWA_PAIRS * QBLK, 2 * KEY_PAD)),
            pl.BlockSpec((None, MEM_TOKENS, MEM_DIM), lambda b, t: (layer, b, 0)),
            pl.BlockSpec((None, MEM_TOKENS, MEM_DIM), lambda b, t: (layer, b, 0)),
            _layer_spec((D_MODEL, D_MODEL), layer),
            _const_spec((1, D_MODEL)),
        ],
        out_specs=(pl.BlockSpec((None, TQ, D_MODEL), lambda b, t: (b, t, 0)),
                   pl.BlockSpec((None, WINDOW, SWA_KV_DIM), lambda b, t: (b, 0, 0)),
                   pl.BlockSpec((None, WINDOW, SWA_KV_DIM), lambda b, t: (b, 0, 0)),
                   pl.BlockSpec((None, CONV_HIST, CONV_DIM), lambda b, t: (b, 0, 0))),
        scratch_shapes=[
            pltpu.VMEM((TQ, D_MODEL), BF16),
            pltpu.VMEM((TQ, A_COLS), F32),
            pltpu.VMEM((TQ, B_COLS), F32),
            pltpu.VMEM((TQ, C_COLS), F32),
            pltpu.VMEM((TQ, D_MODEL), BF16),
            pltpu.VMEM((2 * TQ, CONV_DIM), F32),
            pltpu.VMEM((TQ, CONV_DIM), F32),
            pltpu.VMEM((4, WINDOW + TQ, 2 * SWA_HEAD_DIM), BF16),
            pltpu.VMEM((4, WINDOW + TQ, 2 * SWA_HEAD_DIM), BF16),
        ],
        compiler_params=pltpu.CompilerParams(dimension_semantics=("arbitrary", "arbitrary"),
                                             vmem_limit_bytes=VMEM_LIMIT),
        name="prompt_layer",
    )(sinks, x, ng, win, vec, wpw, perm, unperm, bias, mk, mv, wo, fg)


def _sample_kernel(sinks_ref, x_ref, ng_ref, win_ref, vec_ref, wpw_ref, bias_ref, ck_ref, cv_ref, cc_ref, mk_ref,
                   mv_ref, wo_ref, fg_ref,
                   out_ref, nk_ref, nv_ref, nc_ref,
                   h_s, za_s, zb_s, zc_s, y_s, c_s, u_s, k4_s, v4_s, *, layer, final, n_seq, seq):
    x = x_ref[...]
    h_s[...] = _rms(x, ng_ref[...]).astype(BF16)
    za_s[...] = jnp.dot(h_s[...], win_ref[:, 0:A_COLS], preferred_element_type=F32)
    zb_s[...] = jnp.dot(h_s[...], win_ref[:, A_COLS:A_COLS + B_COLS], preferred_element_type=F32)
    zc_s[...] = jnp.dot(h_s[...], win_ref[:, A_COLS + B_COLS:IN_COLS], preferred_element_type=F32)

    off0 = HIST_ROWS - CONV_HIST
    u_s[0:off0, :] = jnp.zeros((off0, CONV_DIM), F32)
    for b in range(n_seq):
        r0 = b * seq
        u_s[off0:HIST_ROWS, :] = cc_ref[b]
        u_s[HIST_ROWS:HIST_ROWS + seq, :] = (za_s[r0:r0 + seq, 0:CONV_DIM]
                                             * _sigmoid(za_s[r0:r0 + seq, CONV_DIM:2 * CONV_DIM]))
        nc_ref[b] = u_s[HIST_ROWS + seq - CONV_HIST:HIST_ROWS + seq, :]
        acc = jnp.zeros((seq, CONV_DIM), F32) + vec_ref[VEC_CONV_B:VEC_CONV_B + 1, :]
        for k in range(CONV_WIDTH):
            acc = acc + vec_ref[k:k + 1, :] * u_s[off0 + k:off0 + k + seq, :]
        c_s[r0:r0 + seq, :] = acc
    ya = _conv_post(c_s[...], vec_ref, wpw_ref[...], za_s[:, 2 * CONV_DIM:3 * CONV_DIM])
    y_s[:, 0:CONV_DIM] = ya.astype(BF16)

    n_keys = WINDOW + seq
    k4_s[:, n_keys:, :] = jnp.zeros((4, KEY_PAD - n_keys, 2 * SWA_HEAD_DIM), BF16)
    v4_s[:, n_keys:, :] = jnp.zeros((4, KEY_PAD - n_keys, 2 * SWA_HEAD_DIM), BF16)
    for b in range(n_seq):
        r0 = b * seq
        kf = jnp.concatenate([ck_ref[b], zb_s[r0:r0 + seq, B_K0:B_K0 + SWA_KV_DIM]], axis=0)
        vf = jnp.concatenate([cv_ref[b], zb_s[r0:r0 + seq, B_V0:B_V0 + SWA_KV_DIM]], axis=0)
        nk_ref[b] = kf[seq:, :]
        nv_ref[b] = vf[seq:, :]
        for i, part in enumerate(_split_halves(kf)):
            k4_s[i, 0:n_keys, :] = part
        for i, part in enumerate(_split_halves(vf)):
            v4_s[i, 0:n_keys, :] = part
        for kv in range(SWA_KV_HEADS):
            c0 = kv * SWA_PAIRS * 2 * SWA_HEAD_DIM
            lhs4 = jnp.concatenate(
                [(zb_s[r0:r0 + seq, c0 + p * 128:c0 + (p + 1) * 128] * SWA_SCALE).astype(BF16)
                 for p in range(SWA_PAIRS)], axis=0)
            if kv == 0:
                kp = jnp.concatenate([k4_s[0], k4_s[3]], axis=0)
                vc = jnp.concatenate([v4_s[0], v4_s[3]], axis=0)
            else:
                kp = jnp.concatenate([k4_s[2], k4_s[1]], axis=0)
                vc = jnp.concatenate([v4_s[2], v4_s[1]], axis=0)
            sink_fn = lambda p, par, kv=kv: sinks_ref[layer, kv * 2 * SWA_PAIRS + 2 * p + par]
            o = _swa_attend(lhs4, kp, vc, bias_ref[kv], sink_fn, seq)
            for p in range(SWA_PAIRS):
                gate = zb_s[r0:r0 + seq, B_G0 + c0 + p * 128:B_G0 + c0 + (p + 1) * 128]
                yb = o[p * seq:(p + 1) * seq, :] * _silu(gate)
                y_s[r0:r0 + seq, CONV_DIM + c0 + p * 128:CONV_DIM + c0 + (p + 1) * 128] = yb.astype(BF16)

    n_mem_rows = MEM_TOKENS * MEM_HEADS
    head_of_row = lax.broadcasted_iota(jnp.int32, (MEM_HEADS * seq, n_mem_rows), 0) // seq
    head_of_col = lax.broadcasted_iota(jnp.int32, (MEM_HEADS * seq, n_mem_rows), 1) % MEM_HEADS
    for b in range(n_seq):
        r0 = b * seq
        yc = _mem_attend_interleaved(zc_s[r0:r0 + seq, 0:MEM_DIM], mk_ref[b].astype(BF16), mv_ref[b].astype(BF16),
                                     head_of_row, head_of_col)
        y_s[r0:r0 + seq, CONV_DIM + SWA_DIM:] = (yc * _silu(zc_s[r0:r0 + seq, MEM_DIM:C_COLS])).astype(BF16)

    o = x + jnp.dot(y_s[...], wo_ref[...], preferred_element_type=F32)
    if final:
        o = _rms(o, fg_ref[...])
    out_ref[...] = o


def _sample_layer(x2d, layer, sinks, ng, win, vec, wpw, bias, ck, cv, cc, mk, mv, wo, fg, final, n_seq, seq):
    rows = n_seq * seq
    n_mem_rows = MEM_TOKENS * MEM_HEADS
    return pl.pallas_call(
        functools.partial(_sample_kernel, layer=layer, final=final, n_seq=n_seq, seq=seq),
        out_shape=(jax.ShapeDtypeStruct((rows, D_MODEL), F32),
                   jax.ShapeDtypeStruct((n_seq, WINDOW, SWA_KV_DIM), F32),
                   jax.ShapeDtypeStruct((n_seq, WINDOW, SWA_KV_DIM), F32),
                   jax.ShapeDtypeStruct((n_seq, CONV_HIST, CONV_DIM), F32)),
        grid=(1,),
        in_specs=[
            pl.BlockSpec(memory_space=pltpu.SMEM),
            _const_spec((rows, D_MODEL)),
            _layer_spec((1, D_MODEL), layer),
            _layer_spec((D_MODEL, IN_COLS), layer),
            _layer_spec((VEC_ROWS, CONV_DIM), layer),
            _layer_spec((CONV_DIM, CONV_DIM), layer),
            _const_spec((SWA_KV_HEADS, SWA_PAIRS * seq, 2 * KEY_PAD)),
            _layer_spec((n_seq, WINDOW, SWA_KV_DIM), layer),
            _layer_spec((n_seq, WINDOW, SWA_KV_DIM), layer),
            _layer_spec((n_seq, CONV_HIST, CONV_DIM), layer),
            _layer_spec((n_seq, n_mem_rows, MEM_HEAD_DIM), layer),
            _layer_spec((n_seq, n_mem_rows, MEM_HEAD_DIM), layer),
            _layer_spec((D_MODEL, D_MODEL), layer),
            _const_spec((1, D_MODEL)),
        ],
        out_specs=(pl.BlockSpec((rows, D_MODEL), lambda i: (0, 0)),
                   pl.BlockSpec((n_seq, WINDOW, SWA_KV_DIM), lambda i: (0, 0, 0)),
                   pl.BlockSpec((n_seq, WINDOW, SWA_KV_DIM), lambda i: (0, 0, 0)),
                   pl.BlockSpec((n_seq, CONV_HIST, CONV_DIM), lambda i: (0, 0, 0))),
        scratch_shapes=[
            pltpu.VMEM((rows, D_MODEL), BF16),
            pltpu.VMEM((rows, A_COLS), F32),
            pltpu.VMEM((rows, B_COLS), F32),
            pltpu.VMEM((rows, C_COLS), F32),
            pltpu.VMEM((rows, D_MODEL), BF16),
            pltpu.VMEM((rows, CONV_DIM), F32),
            pltpu.VMEM((HIST_ROWS + seq, CONV_DIM), F32),
            pltpu.VMEM((4, KEY_PAD, 2 * SWA_HEAD_DIM), BF16),
            pltpu.VMEM((4, KEY_PAD, 2 * SWA_HEAD_DIM), BF16),
        ],
        compiler_params=pltpu.CompilerParams(dimension_semantics=("arbitrary",), vmem_limit_bytes=VMEM_LIMIT),
        name="sample_layer",
    )(sinks, x2d, ng, win, vec, wpw, bias, ck, cv, cc, mk, mv, wo, fg)


def kernel(x_prompt, x_sample, mem_prompt, cache_swa_k, cache_swa_v, cache_conv, cache_mem_k, cache_mem_v, norm_g,
           w_in, conv_w, conv_b, conv_ln_g, conv_ln_b, w_pw, b_pw, swa_sinks, mem_norm_g, w_mem_kv, rel_bias,
           w_out, final_norm_g):
    B, T, _ = x_prompt.shape
    n_seq, seq, _ = x_sample.shape
    L = cache_swa_k.shape[2]
    assert T % TQ == 0 and L == WINDOW and seq % 16 == 0 and WINDOW + seq <= KEY_PAD

    win = w_in.astype(BF16)
    wpw = w_pw.astype(BF16)
    wo = w_out.astype(BF16)
    wmem = w_mem_kv.astype(BF16)
    vec = jnp.concatenate(
        [conv_w, conv_b[:, None, :], conv_ln_g[:, None, :], conv_ln_b[:, None, :], b_pw[:, None, :],
         jnp.zeros((DEPTH, VEC_ROWS - CONV_WIDTH - 4, CONV_DIM), F32)], axis=1)
    ng = norm_g.reshape(DEPTH, 1, D_MODEL)
    fg = final_norm_g.reshape(1, D_MODEL)

    perm_np = _regroup_matrix()
    perm = jnp.asarray(perm_np, BF16)
    unperm = jnp.asarray(perm_np.T, BF16)
    bias_p = _bias_table(rel_bias, QBLK, _prompt_valid())
    bias_s = _bias_table(rel_bias, seq, _sample_valid(seq, L + seq))

    mk_all, mv_all = _memory_kv(mem_prompt.reshape(B * MEM_TOKENS, D_MODEL), mem_norm_g.reshape(DEPTH, 1, D_MODEL),
                                wmem)

    ck = cache_swa_k.reshape(DEPTH, n_seq, L, SWA_KV_DIM)
    cv = cache_swa_v.reshape(DEPTH, n_seq, L, SWA_KV_DIM)
    cmk = cache_mem_k.reshape(DEPTH, n_seq, MEM_TOKENS * MEM_HEADS, MEM_HEAD_DIM)
    cmv = cache_mem_v.reshape(DEPTH, n_seq, MEM_TOKENS * MEM_HEADS, MEM_HEAD_DIM)

    hp = x_prompt
    hs = x_sample.reshape(n_seq * seq, D_MODEL)
    p_k, p_v, p_c, s_k, s_v, s_c = [], [], [], [], [], []
    for l in range(DEPTH):
        final = l == DEPTH - 1
        hp, nk, nv, nc = _prompt_layer(hp, l, swa_sinks, ng, win, vec, wpw, perm, unperm, bias_p, mk_all, mv_all, wo,
                                       fg, final)
        p_k.append(nk)
        p_v.append(nv)
        p_c.append(nc)
        hs, nk, nv, nc = _sample_layer(hs, l, swa_sinks, ng, win, vec, wpw, bias_s, ck, cv, cache_conv, cmk, cmv,
                                       wo, fg, final, n_seq, seq)
        s_k.append(nk)
        s_v.append(nv)
        s_c.append(nc)

    kv_shape = (DEPTH, -1, WINDOW, SWA_KV_HEADS, SWA_HEAD_DIM)
    mem_shape = (DEPTH, B, MEM_TOKENS, MEM_HEADS, MEM_HEAD_DIM)
    return (hp, hs.reshape(n_seq, seq, D_MODEL),
            jnp.stack(p_k).reshape(kv_shape), jnp.stack(p_v).reshape(kv_shape), jnp.stack(p_c),
            mk_all.reshape(mem_shape), mv_all.reshape(mem_shape),
            jnp.stack(s_k).reshape(kv_shape), jnp.stack(s_v).reshape(kv_shape), jnp.stack(s_c))
```
